```python
import math
import jax, jax.numpy as jnp
from jax import lax
import numpy as np

D_MODEL = 1024
BATCH = 32
SEQ = 2048
DEPTH = 4

N_MIXERS = 2
N_ATTN_LAYERS = (DEPTH + N_MIXERS - 1) // N_MIXERS
N_REC_LAYERS = DEPTH // N_MIXERS
HEAD_DIM = 64
ATTN_HEADS = D_MODEL // (2 * HEAD_DIM)
ATTN_WIDTH = ATTN_HEADS * 2 * HEAD_DIM
Q_BLOCK = 128
SUBLN_EPS = 1e-5
D_RNN = D_MODEL
RNN_HEADS = 4
RNN_BLOCK = D_RNN // RNN_HEADS
CONV_WIDTH = 4
RG_LRU_C = 8.0
MIN_RAD = 0.9
MAX_RAD = 0.999
N_GROUPS = 4
EXPERTS_PER_GROUP = 4
N_EXPERTS = N_GROUPS * EXPERTS_PER_GROUP
TOP_K = 2
D_EXPERT = 512
NORM_EPS = 1e-6

kernel_name = "hybrid_diffattn_rglru_hmoe_adaln"


def rms_norm(x, g, eps=NORM_EPS):
    xf = x.astype(jnp.float32)
    y = xf * lax.rsqrt(jnp.mean(xf * xf, axis=-1, keepdims=True) + eps)
    return (y * g.astype(jnp.float32)).astype(x.dtype)


def modulate(h, shift, scale):
    return h * (1 + scale[:, None, :]) + shift[:, None, :]


def lambda_init_fn(layer):
    return 0.8 - 0.6 * math.exp(-0.3 * layer)


def diff_attention(h, w_qkv, lam_vec, subln_g, w_o, lambda_init):
    B, S, _ = h.shape
    qkv = h @ w_qkv
    q, k, v = jnp.split(qkv, 3, axis=-1)
    q = q.reshape(B, S, ATTN_HEADS, 2, HEAD_DIM)
    k = k.reshape(B, S, ATTN_HEADS, 2, HEAD_DIM)
    v = v.reshape(B, S, ATTN_HEADS, 2 * HEAD_DIM)
    lv = lam_vec.astype(jnp.float32)
    lam = jnp.exp(jnp.sum(lv[0] * lv[1])) - jnp.exp(jnp.sum(lv[2] * lv[3])) + lambda_init
    scale = HEAD_DIM ** -0.5
    outs = []
    for blk in range(S // Q_BLOCK):
        q0 = blk * Q_BLOCK
        kv_len = q0 + Q_BLOCK
        s = jnp.einsum('bqhnd,bkhnd->bhnqk', q[:, q0:kv_len], k[:, :kv_len]).astype(jnp.float32) * scale
        mask = (q0 + jnp.arange(Q_BLOCK))[:, None] >= jnp.arange(kv_len)[None, :]
        p = jax.nn.softmax(jnp.where(mask, s, -jnp.inf), axis=-1)
        p = p[:, :, 0] - lam * p[:, :, 1]
        outs.append(jnp.einsum('bhqk,bkhe->bqhe', p.astype(v.dtype), v[:, :kv_len]))
    o = jnp.concatenate(outs, axis=1)
    o = rms_norm(o, subln_g, SUBLN_EPS) * (1 - lambda_init)
    return o.reshape(B, S, ATTN_WIDTH) @ w_o


def causal_depthwise_conv(x, w, b):
    y = lax.conv_general_dilated(x, w[:, None, :], window_strides=(1,), padding=[(CONV_WIDTH - 1, 0)],
                                 dimension_numbers=('NWC', 'WIO', 'NWC'), feature_group_count=x.shape[-1])
    return y + b


def _linear_combine(left, right):
    a1, b1 = left
    a2, b2 = right
    return a1 * a2, a2 * b1 + b2


def recurrent_block(h, w_in, conv_w, conv_b, gate_w, gate_b, a_param, w_out):
    B, S, _ = h.shape
    y, xr = jnp.split(h @ w_in, 2, axis=-1)
    xr = causal_depthwise_conv(xr, conv_w, conv_b)
    gates = jnp.einsum('bshi,hij->bshj', xr.reshape(B, S, RNN_HEADS, RNN_BLOCK), gate_w) + gate_b
    gates = jax.nn.sigmoid(gates.astype(jnp.float32))
    r = gates[..., :RNN_BLOCK].reshape(B, S, D_RNN)
    i = gates[..., RNN_BLOCK:].reshape(B, S, D_RNN)
    log_a = -RG_LRU_C * r * jax.nn.softplus(-a_param.astype(jnp.float32))
    a = jnp.exp(log_a)
    b = jnp.sqrt(-jnp.expm1(2 * log_a)) * (i * xr.astype(jnp.float32))
    _, hs = lax.associative_scan(_linear_combine, (a, b), axis=1)
    return (jax.nn.gelu(y, approximate=True) * hs.astype(y.dtype)) @ w_out


def hier_moe(h, w_group, b_group, w_expert, b_expert, w13, w2):
    B, S, D = h.shape
    t = h.reshape(-1, D)
    g_logits = (t @ w_group).astype(jnp.float32) + b_group
    g_idx = jnp.argmax(g_logits, axis=-1)
    g_w = jnp.take_along_axis(jax.nn.softmax(g_logits, axis=-1), g_idx[:, None], axis=-1)[:, 0]
    e_logits = ((t @ w_expert).astype(jnp.float32) + b_expert).reshape(-1, N_GROUPS, EXPERTS_PER_GROUP)
    e_in = jnp.take_along_axis(e_logits, g_idx[:, None, None], axis=1)[:, 0]
    top_v, top_i = lax.top_k(e_in, TOP_K)
    top_w = jax.nn.softmax(top_v, axis=-1) * g_w[:, None]
    expert_id = g_idx[:, None] * EXPERTS_PER_GROUP + top_i
    combine = jnp.sum(jax.nn.one_hot(expert_id, N_EXPERTS, dtype=jnp.float32) * top_w[..., None], axis=1)
    combine = combine.astype(t.dtype)
    out = jnp.zeros_like(t)
    for e in range(N_EXPERTS):
        u = t @ w13[e]
        act = jax.nn.silu(u[:, :D_EXPERT]) * u[:, D_EXPERT:]
        out = out + combine[:, e:e + 1] * (act @ w2[e])
    return out.reshape(B, S, D)


def setup_inputs(seed: int = 0) -> dict:
    key = jax.random.key(seed)
    ks = jax.random.split(key, 24)
    f32 = jnp.float32

    def nrm(k, shape, scale):
        return jax.random.normal(k, shape, f32) * scale

    u = jax.random.uniform(ks[16], (N_REC_LAYERS, D_RNN), f32, MIN_RAD ** 2, MAX_RAD ** 2)
    return {
        "x": nrm(ks[0], (BATCH, SEQ, D_MODEL), 1.0),
        "c": nrm(ks[1], (BATCH, D_MODEL), 1.0),
        "norm_mix": 1.0 + nrm(ks[2], (DEPTH, D_MODEL), 0.02),
        "norm_ffn": 1.0 + nrm(ks[3], (DEPTH, D_MODEL), 0.02),
        "final_norm": 1.0 + nrm(ks[4], (D_MODEL,), 0.02),
        "ada_w": nrm(ks[5], (DEPTH, D_MODEL, 6 * D_MODEL), 0.5 * D_MODEL ** -0.5),
        "ada_b": nrm(ks[6], (DEPTH, 6 * D_MODEL), 0.02),
        "attn_w_qkv": nrm(ks[7], (N_ATTN_LAYERS, D_MODEL, 3 * ATTN_WIDTH), D_MODEL ** -0.5),
        "attn_lambda": nrm(ks[8], (N_ATTN_LAYERS, 4, HEAD_DIM), 0.1),
        "attn_subln": 1.0 + nrm(ks[9], (N_ATTN_LAYERS, 2 * HEAD_DIM), 0.02),
        "attn_w_o": nrm(ks[10], (N_ATTN_LAYERS, ATTN_WIDTH, D_MODEL), ATTN_WIDTH ** -0.5),
        "rec_w_in": nrm(ks[11], (N_REC_LAYERS, D_MODEL, 2 * D_RNN), D_MODEL ** -0.5),
        "rec_conv_w": nrm(ks[12], (N_REC_LAYERS, CONV_WIDTH, D_RNN), CONV_WIDTH ** -0.5),
        "rec_conv_b": nrm(ks[13], (N_REC_LAYERS, D_RNN), 0.01),
        "rec_gate_w": nrm(ks[14], (N_REC_LAYERS, RNN_HEADS, RNN_BLOCK, 2 * RNN_BLOCK), RNN_BLOCK ** -0.5),
        "rec_gate_b": nrm(ks[15], (N_REC_LAYERS, RNN_HEADS, 2 * RNN_BLOCK), 0.01),
        "rec_a_param": -jnp.log(u ** -0.5 - 1.0),
        "rec_w_out": nrm(ks[17], (N_REC_LAYERS, D_RNN, D_MODEL), D_RNN ** -0.5),
        "moe_w_group": nrm(ks[18], (DEPTH, D_MODEL, N_GROUPS), D_MODEL ** -0.5),
        "moe_b_group": nrm(ks[19], (DEPTH, N_GROUPS), 0.01),
        "moe_w_expert": nrm(ks[20], (DEPTH, D_MODEL, N_EXPERTS), D_MODEL ** -0.5),
        "moe_b_expert": nrm(ks[21], (DEPTH, N_EXPERTS), 0.01),
        "moe_w13": nrm(ks[22], (DEPTH, N_EXPERTS, D_MODEL, 2 * D_EXPERT), D_MODEL ** -0.5),
        "moe_w2": nrm(ks[23], (DEPTH, N_EXPERTS, D_EXPERT, D_MODEL), D_EXPERT ** -0.5),
    }


def reference(x, c, norm_mix, norm_ffn, final_norm, ada_w, ada_b,
              attn_w_qkv, attn_lambda, attn_subln, attn_w_o,
              rec_w_in, rec_conv_w, rec_conv_b, rec_gate_w, rec_gate_b, rec_a_param, rec_w_out,
              moe_w_group, moe_b_group, moe_w_expert, moe_b_expert, moe_w13, moe_w2):
    cond = jax.nn.silu(c)
    for layer in range(DEPTH):
        mod = cond @ ada_w[layer] + ada_b[layer]
        sh1, sc1, g1, sh2, sc2, g2 = jnp.split(mod, 6, axis=-1)
        hm = modulate(rms_norm(x, norm_mix[layer]), sh1, sc1)
        j = layer // N_MIXERS
        if layer % N_MIXERS == 0:
            mix = diff_attention(hm, attn_w_qkv[j], attn_lambda[j], attn_subln[j], attn_w_o[j],
                                 lambda_init_fn(layer))
        else:
            mix = recurrent_block(hm, rec_w_in[j], rec_conv_w[j], rec_conv_b[j], rec_gate_w[j],
                                  rec_gate_b[j], rec_a_param[j], rec_w_out[j])
        x = x + g1[:, None, :] * mix
        hf = modulate(rms_norm(x, norm_ffn[layer]), sh2, sc2)
        x = x + g2[:, None, :] * hier_moe(hf, moe_w_group[layer], moe_b_group[layer], moe_w_expert[layer],
                                          moe_b_expert[layer], moe_w13[layer], moe_w2[layer])
    return rms_norm(x, final_norm)
```

```python
import functools
import math

import jax
import jax.numpy as jnp
from jax import lax
from jax.experimental import pallas as pl
from jax.experimental.pallas import tpu as pltpu

D_MODEL = 1024
BATCH = 32
SEQ = 2048
TOKENS = BATCH * SEQ
DEPTH = 4
N_MIXERS = 2
HEAD_DIM = 64
ATTN_HEADS = D_MODEL // (2 * HEAD_DIM)
ATTN_WIDTH = ATTN_HEADS * 2 * HEAD_DIM
SUBLN_EPS = 1e-5
D_RNN = D_MODEL
RNN_HEADS = 4
RNN_BLOCK = D_RNN // RNN_HEADS
CONV_WIDTH = 4
RG_LRU_C = 8.0
N_GROUPS = 4
EXPERTS_PER_GROUP = 4
N_EXPERTS = N_GROUPS * EXPERTS_PER_GROUP
D_EXPERT = 512
NORM_EPS = 1e-6

LANES = 128
SUBLANES = 8
ROUTER_LANES = LANES
VMEM_LIMIT = 56 * 1024 * 1024

F32 = jnp.float32
BF16 = jnp.bfloat16


def _params(semantics):
    return pltpu.CompilerParams(dimension_semantics=semantics, vmem_limit_bytes=VMEM_LIMIT)


def _rms_mod(x, g, shift, scale, eps=NORM_EPS):
    y = x * lax.rsqrt(jnp.mean(x * x, axis=-1, keepdims=True) + eps) * g
    return y * (1.0 + scale) + shift


def _ada_kernel(c_ref, w_ref, b_ref, o_ref):
    c = c_ref[...]
    cond = (c * jax.nn.sigmoid(c)).astype(BF16)
    o_ref[...] = jnp.dot(cond, w_ref[...].astype(BF16), preferred_element_type=F32) + b_ref[...]


def _ada_mod(c, ada_w, ada_b):
    tn = 1536
    n = 6 * D_MODEL
    return pl.pallas_call(
        _ada_kernel,
        out_shape=jax.ShapeDtypeStruct((DEPTH, BATCH, n), F32),
        grid=(DEPTH, n // tn),
        in_specs=[
            pl.BlockSpec((BATCH, D_MODEL), lambda l, j: (0, 0)),
            pl.BlockSpec((None, D_MODEL, tn), lambda l, j: (l, 0, j)),
            pl.BlockSpec((None, 1, tn), lambda l, j: (l, 0, j)),
        ],
        out_specs=pl.BlockSpec((None, BATCH, tn), lambda l, j: (l, 0, j)),
        compiler_params=_params(("arbitrary", "arbitrary")),
        name="ada_mod",
    )(c, ada_w, ada_b.reshape(DEPTH, 1, n))


def _norm_proj_kernel(x_ref, g_ref, mod_ref, w_ref, o_ref, *, shift_row, scale_row, tn):
    h = _rms_mod(x_ref[...], g_ref[...], mod_ref[shift_row:shift_row + 1, :],
                 mod_ref[scale_row:scale_row + 1, :]).astype(BF16)
    for j in range(o_ref.shape[1] // tn):
        o_ref[:, j * tn:(j + 1) * tn] = jnp.dot(
            h, w_ref[:, j * tn:(j + 1) * tn], preferred_element_type=F32).astype(o_ref.dtype)


def _norm_proj(x, g, mod, w, *, shift_row, scale_row, tm=512, tn=1024):
    n = w.shape[1]
    return pl.pallas_call(
        functools.partial(_norm_proj_kernel, shift_row=shift_row, scale_row=scale_row, tn=tn),
        out_shape=jax.ShapeDtypeStruct((TOKENS, n), BF16),
        grid=(TOKENS // tm,),
        in_specs=[
            pl.BlockSpec((tm, D_MODEL), lambda i: (i, 0)),
            pl.BlockSpec((1, D_MODEL), lambda i: (0, 0)),
            pl.BlockSpec((None, 6, D_MODEL), lambda i: (i // (SEQ // tm), 0, 0)),
            pl.BlockSpec((D_MODEL, n), lambda i: (0, 0)),
        ],
        out_specs=pl.BlockSpec((tm, n), lambda i: (i, 0)),
        compiler_params=_params(("arbitrary",)),
        name="norm_proj",
    )(x, g.reshape(1, D_MODEL), mod, w)


def _attn_kernel(lam_ref, sg_ref, q_ref, k_ref, v_ref, o_ref, m_scr, l_scr, acc_scr,
                 *, tq, lambda_init):
    qi = pl.program_id(2)
    lv = lam_ref[...]
    lam = (jnp.exp(jnp.sum(lv[0:1] * lv[1:2], keepdims=True))
           - jnp.exp(jnp.sum(lv[2:3] * lv[3:4], keepdims=True)) + lambda_init)

    q = q_ref[...] * jnp.asarray(HEAD_DIM ** -0.5, BF16)
    lane = lax.broadcasted_iota(jnp.int32, q.shape, 1)
    q_maps = (jnp.where(lane < HEAD_DIM, q, jnp.zeros_like(q)),
              jnp.where(lane >= HEAD_DIM, q, jnp.zeros_like(q)))

    m_scr[...] = jnp.full(m_scr.shape, -jnp.inf, F32)
    l_scr[...] = jnp.zeros(l_scr.shape, F32)
    acc_scr[...] = jnp.zeros(acc_scr.shape, F32)

    def step(j, masked):
        start = pl.multiple_of(j * tq, tq)
        kb = k_ref[pl.ds(start, tq), :]
        vb = v_ref[pl.ds(start, tq), :]
        for mi in range(2):
            s = lax.dot_general(q_maps[mi], kb, (((1,), (1,)), ((), ())),
                                preferred_element_type=F32)
            if masked:
                row = lax.broadcasted_iota(jnp.int32, s.shape, 0)
                col = lax.broadcasted_iota(jnp.int32, s.shape, 1)
                s = jnp.where(row >= col, s, -jnp.inf)
            m_prev = m_scr[mi]
            m_new = jnp.maximum(m_prev, jnp.max(s, axis=-1, keepdims=True))
            alpha = jnp.exp(m_prev - m_new)
            p = jnp.exp(s - m_new[:, :1])
            l_scr[mi] = alpha * l_scr[mi] + jnp.sum(p, axis=-1, keepdims=True)
            acc_scr[mi] = alpha * acc_scr[mi] + jnp.dot(
                p.astype(BF16), vb, preferred_element_type=F32)
            m_scr[mi] = m_new

    def body(j, carry):
        step(j, False)
        return carry

    lax.fori_loop(0, qi, body, 0)
    step(qi, True)

    o = acc_scr[0] / l_scr[0] - lam * (acc_scr[1] / l_scr[1])
    o = o * lax.rsqrt(jnp.mean(o * o, axis=-1, keepdims=True) + SUBLN_EPS) * sg_ref[...]
    o_ref[...] = (o * (1.0 - lambda_init)).astype(o_ref.dtype)


def _diff_attention(qkv, lam_vec, subln_g, lambda_init, *, tq=256):
    nq = SEQ // tq
    hw = 2 * HEAD_DIM
    assert hw == LANES
    return pl.pallas_call(
        functools.partial(_attn_kernel, tq=tq, lambda_init=lambda_init),
        out_shape=jax.ShapeDtypeStruct((TOKENS, ATTN_WIDTH), BF16),
        grid=(BATCH, ATTN_HEADS, nq),
        in_specs=[
            pl.BlockSpec((4, HEAD_DIM), lambda b, h, i: (0, 0)),
            pl.BlockSpec((1, hw), lambda b, h, i: (0, 0)),
            pl.BlockSpec((tq, hw), lambda b, h, i: (b * nq + i, h)),
            pl.BlockSpec((SEQ, hw), lambda b, h, i: (b, ATTN_HEADS + h)),
            pl.BlockSpec((SEQ, hw), lambda b, h, i: (b, 2 * ATTN_HEADS + h)),
        ],
        out_specs=pl.BlockSpec((tq, hw), lambda b, h, i: (b * nq + i, h)),
        scratch_shapes=[
            pltpu.VMEM((2, tq, LANES), F32),
            pltpu.VMEM((2, tq, LANES), F32),
            pltpu.VMEM((2, tq, hw), F32),
        ],
        compiler_params=_params(("arbitrary", "arbitrary", "arbitrary")),
        name="diff_attention",
    )(lam_vec, subln_g.reshape(1, hw), qkv, qkv, qkv)


def _proj_residual_kernel(a_ref, w_ref, x_ref, mod_ref, o_ref, *, gate_row):
    y = jnp.dot(a_ref[...], w_ref[...], preferred_element_type=F32)
    o_ref[...] = x_ref[...] + mod_ref[gate_row:gate_row + 1, :] * y


def _proj_residual(a, w, x, mod, *, gate_row, tm=512):
    k = a.shape[1]
    return pl.pallas_call(
        functools.partial(_proj_residual_kernel, gate_row=gate_row),
        out_shape=jax.ShapeDtypeStruct((TOKENS, D_MODEL), F32),
        grid=(TOKENS // tm,),
        in_specs=[
            pl.BlockSpec((tm, k), lambda i: (i, 0)),
            pl.BlockSpec((k, D_MODEL), lambda i: (0, 0)),
            pl.BlockSpec((tm, D_MODEL), lambda i: (i, 0)),
            pl.BlockSpec((None, 6, D_MODEL), lambda i: (i // (SEQ // tm), 0, 0)),
        ],
        out_specs=pl.BlockSpec((tm, D_MODEL), lambda i: (i, 0)),
        compiler_params=_params(("arbitrary",)),
        name="proj_residual",
    )(a, w, x, mod)


def _gelu_tanh(y):
    c = math.sqrt(2.0 / math.pi)
    return 0.5 * y * (1.0 + jnp.tanh(c * (y + 0.044715 * (y * y * y))))


def _rec_kernel(x_ref, g_ref, mod_ref, w_in_ref, conv_w_ref, conv_b_ref, gate_w_ref, gate_b_ref,
                a_param_ref, w_out_ref, o_ref, ext_scr, a_scr, b_scr, h_scr, carry_scr, *, tm):
    t = pl.program_id(1)

    @pl.when(t == 0)
    def _():
        carry_scr[...] = jnp.zeros(carry_scr.shape, F32)
        ext_scr[0:SUBLANES, :] = jnp.zeros((SUBLANES, D_RNN), F32)

    x = x_ref[...]
    hm = _rms_mod(x, g_ref[...], mod_ref[0:1, :], mod_ref[1:2, :]).astype(BF16)
    y = jnp.dot(hm, w_in_ref[:, :D_RNN], preferred_element_type=F32)
    xr = jnp.dot(hm, w_in_ref[:, D_RNN:], preferred_element_type=F32)

    ext_scr[SUBLANES:SUBLANES + tm, :] = xr
    xc = conv_b_ref[...] + conv_w_ref[CONV_WIDTH - 1:CONV_WIDTH, :] * xr
    for k in range(CONV_WIDTH - 1):
        back = CONV_WIDTH - 1 - k
        xc = xc + conv_w_ref[k:k + 1, :] * ext_scr[SUBLANES - back:SUBLANES - back + tm, :]
    ext_scr[0:SUBLANES, :] = xr[tm - SUBLANES:, :]

    z = -a_param_ref[...]
    softplus = jnp.maximum(z, 0.0) + jnp.log(1.0 + jnp.exp(-jnp.abs(z)))
    xcb = xc.astype(BF16)
    for hd in range(RNN_HEADS):
        sl = slice(hd * RNN_BLOCK, (hd + 1) * RNN_BLOCK)
        gates = jnp.dot(xcb[:, sl], gate_w_ref[hd], preferred_element_type=F32) + gate_b_ref[hd]
        gates = jax.nn.sigmoid(gates)
        r = gates[:, :RNN_BLOCK]
        i = gates[:, RNN_BLOCK:]
        log_a = -RG_LRU_C * r * softplus[:, sl]
        a = jnp.exp(log_a)
        a_scr[:, sl] = a
        b_scr[:, sl] = jnp.sqrt(1.0 - a * a) * (i * xc[:, sl])

    sub = lax.broadcasted_iota(jnp.int32, (SUBLANES, D_RNN), 0)

    def scan_body(g, carry):
        r0 = pl.multiple_of(g * SUBLANES, SUBLANES)
        a = a_scr[pl.ds(r0, SUBLANES), :]
        b = b_scr[pl.ds(r0, SUBLANES), :]
        for d in (1, 2, 4):
            keep = sub >= d
            a_sh = jnp.where(keep, pltpu.roll(a, d, axis=0), 1.0)
            b_sh = jnp.where(keep, pltpu.roll(b, d, axis=0), 0.0)
            b = a * b_sh + b
            a = a * a_sh
        h = a * carry + b
        h_scr[pl.ds(r0, SUBLANES), :] = h
        return jnp.broadcast_to(h[SUBLANES - 1:SUBLANES, :], (SUBLANES, D_RNN))

    carry_scr[...] = lax.fori_loop(0, tm // SUBLANES, scan_body, carry_scr[...])

    mixed = (_gelu_tanh(y) * h_scr[...]).astype(BF16)
    out = jnp.dot(mixed, w_out_ref[...], preferred_element_type=F32)
    o_ref[...] = x + mod_ref[2:3, :] * out


def _recurrent_layer(x, g, mod, w_in, conv_w, conv_b, gate_w, gate_b, a_param, w_out, *, tm=512):
    nt = SEQ // tm
    const2 = lambda b, t: (0, 0)
    const3 = lambda b, t: (0, 0, 0)
    return pl.pallas_call(
        functools.partial(_rec_kernel, tm=tm),
        out_shape=jax.ShapeDtypeStruct((TOKENS, D_MODEL), F32),
        grid=(BATCH, nt),
        in_specs=[
            pl.BlockSpec((tm, D_MODEL), lambda b, t: (b * nt + t, 0)),
            pl.BlockSpec((1, D_MODEL), const2),
            pl.BlockSpec((None, 6, D_MODEL), lambda b, t: (b, 0, 0)),
            pl.BlockSpec((D_MODEL, 2 * D_RNN), const2),
            pl.BlockSpec((CONV_WIDTH, D_RNN), const2),
            pl.BlockSpec((1, D_RNN), const2),
            pl.BlockSpec((RNN_HEADS, RNN_BLOCK, 2 * RNN_BLOCK), const3),
            pl.BlockSpec((RNN_HEADS, 1, 2 * RNN_BLOCK), const3),
            pl.BlockSpec((1, D_RNN), const2),
            pl.BlockSpec((D_RNN, D_MODEL), const2),
        ],
        out_specs=pl.BlockSpec((tm, D_MODEL), lambda b, t: (b * nt + t, 0)),
        scratch_shapes=[
            pltpu.VMEM((tm + SUBLANES, D_RNN), F32),
            pltpu.VMEM((tm, D_RNN), F32),
            pltpu.VMEM((tm, D_RNN), F32),
            pltpu.VMEM((tm, D_RNN), F32),
            pltpu.VMEM((SUBLANES, D_RNN), F32),
        ],
        compiler_params=_params(("arbitrary", "arbitrary")),
        name="recurrent_layer",
    )(x, g.reshape(1, D_MODEL), mod, w_in, conv_w, conv_b.reshape(1, D_RNN), gate_w,
      gate_b.reshape(RNN_HEADS, 1, 2 * RNN_BLOCK), a_param.reshape(1, D_RNN), w_out)


def _router_kernel(x_ref, g_ref, mod_ref, w_hi_ref, w_lo_ref, b_ref, hf_ref, comb_ref):
    hf = _rms_mod(x_ref[...], g_ref[...], mod_ref[3:4, :], mod_ref[4:5, :])
    hi = hf.astype(BF16)
    hf_ref[...] = hi
    lo = (hf - hi.astype(F32)).astype(BF16)
    logits = (jnp.dot(hi, w_hi_ref[...], preferred_element_type=F32)
              + jnp.dot(lo, w_hi_ref[...], preferred_element_type=F32)
              + jnp.dot(hi, w_lo_ref[...], preferred_element_type=F32)) + b_ref[...]

    lane = lax.broadcasted_iota(jnp.int32, logits.shape, 1)
    neg = -jnp.inf
    big = ROUTER_LANES

    def masked_max(mask):
        return jnp.max(jnp.where(mask, logits, neg), axis=-1, keepdims=True)

    def first_lane(mask):
        return jnp.min(jnp.where(mask, lane, big), axis=-1, keepdims=True)

    gmask = lane < N_GROUPS
    gmax = masked_max(gmask)
    g_idx = first_lane(gmask & (logits == gmax))
    g_w = 1.0 / jnp.sum(jnp.where(gmask, jnp.exp(logits - gmax), 0.0), axis=-1, keepdims=True)

    e_lo = N_GROUPS + g_idx * EXPERTS_PER_GROUP
    emask = (lane >= e_lo) & (lane < e_lo + EXPERTS_PER_GROUP)
    v1 = masked_max(emask)
    i1 = first_lane(emask & (logits == v1))
    emask2 = emask & (lane != i1)
    v2 = masked_max(emask2)
    i2 = first_lane(emask2 & (logits == v2))
    e2 = jnp.exp(v2 - v1)
    p1 = 1.0 / (1.0 + e2)
    p2 = e2 * p1
    comb_ref[...] = jnp.where(lane == i1, p1 * g_w, jnp.where(lane == i2, p2 * g_w, 0.0))


def _router(x, g, mod, w_hi, w_lo, bias, *, tm=512):
    return pl.pallas_call(
        _router_kernel,
        out_shape=(jax.ShapeDtypeStruct((TOKENS, D_MODEL), BF16),
                   jax.ShapeDtypeStruct((TOKENS, ROUTER_LANES), F32)),
        grid=(TOKENS // tm,),
        in_specs=[
            pl.BlockSpec((tm, D_MODEL), lambda i: (i, 0)),
            pl.BlockSpec((1, D_MODEL), lambda i: (0, 0)),
            pl.BlockSpec((None, 6, D_MODEL), lambda i: (i // (SEQ // tm), 0, 0)),
            pl.BlockSpec((D_MODEL, ROUTER_LANES), lambda i: (0, 0)),
            pl.BlockSpec((D_MODEL, ROUTER_LANES), lambda i: (0, 0)),
            pl.BlockSpec((1, ROUTER_LANES), lambda i: (0, 0)),
        ],
        out_specs=(pl.BlockSpec((tm, D_MODEL), lambda i: (i, 0)),
                   pl.BlockSpec((tm, ROUTER_LANES), lambda i: (i, 0))),
        compiler_params=_params(("arbitrary",)),
        name="moe_router",
    )(x, g.reshape(1, D_MODEL), mod, w_hi, w_lo, bias)


def _moe_dense_kernel(hf_ref, comb_ref, w13_ref, w2_ref, x_ref, mod_ref, o_ref, acc_scr):
    e = pl.program_id(1)

    @pl.when(e == 0)
    def _():
        acc_scr[...] = jnp.zeros(acc_scr.shape, F32)

    u = jnp.dot(hf_ref[...], w13_ref[...], preferred_element_type=F32)
    u1 = u[:, :D_EXPERT]
    act = (u1 * jax.nn.sigmoid(u1) * u[:, D_EXPERT:]).astype(BF16)
    y = jnp.dot(act, w2_ref[...], preferred_element_type=F32)
    comb = comb_ref[...]
    lane = lax.broadcasted_iota(jnp.int32, comb.shape, 1)
    cw = jnp.sum(jnp.where(lane == e + N_GROUPS, comb, 0.0), axis=-1, keepdims=True)
    acc_scr[...] += cw * y

    @pl.when(e == N_EXPERTS - 1)
    def _():
        o_ref[...] = x_ref[...] + mod_ref[5:6, :] * acc_scr[...]


def _moe_dense(hf, comb, w13, w2, x, mod, *, tm=1024):
    return pl.pallas_call(
        _moe_dense_kernel,
        out_shape=jax.ShapeDtypeStruct((TOKENS, D_MODEL), F32),
        grid=(TOKENS // tm, N_EXPERTS),
        in_specs=[
            pl.BlockSpec((tm, D_MODEL), lambda i, e: (i, 0)),
            pl.BlockSpec((tm, ROUTER_LANES), lambda i, e: (i, 0)),
            pl.BlockSpec((None, D_MODEL, 2 * D_EXPERT), lambda i, e: (e, 0, 0)),
            pl.BlockSpec((None, D_EXPERT, D_MODEL), lambda i, e: (e, 0, 0)),
            pl.BlockSpec((tm, D_MODEL), lambda i, e: (i, 0)),
            pl.BlockSpec((None, 6, D_MODEL), lambda i, e: (i // (SEQ // tm), 0, 0)),
        ],
        out_specs=pl.BlockSpec((tm, D_MODEL), lambda i, e: (i, 0)),
        scratch_shapes=[pltpu.VMEM((tm, D_MODEL), F32)],
        compiler_params=_params(("arbitrary", "arbitrary")),
        name="moe_dense",
    )(hf, comb, w13, w2, x, mod)


def _final_norm_kernel(x_ref, g_ref, o_ref):
    x = x_ref[...]
    o_ref[...] = x * lax.rsqrt(jnp.mean(x * x, axis=-1, keepdims=True) + NORM_EPS) * g_ref[...]


def _final_norm(x, g, *, tm=1024):
    return pl.pallas_call(
        _final_norm_kernel,
        out_shape=jax.ShapeDtypeStruct((TOKENS, D_MODEL), F32),
        grid=(TOKENS // tm,),
        in_specs=[pl.BlockSpec((tm, D_MODEL), lambda i: (i, 0)),
                  pl.BlockSpec((1, D_MODEL), lambda i: (0, 0))],
        out_specs=pl.BlockSpec((tm, D_MODEL), lambda i: (i, 0)),
        compiler_params=_params(("arbitrary",)),
        name="final_norm",
    )(x, g.reshape(1, D_MODEL))


def _lambda_init(layer):
    return 0.8 - 0.6 * math.exp(-0.3 * layer)


def _split_bf16(w):
    hi = w.astype(BF16)
    return hi, (w - hi.astype(F32)).astype(BF16)


def kernel(x, c, norm_mix, norm_ffn, final_norm, ada_w, ada_b, attn_w_qkv, attn_lambda, attn_subln, attn_w_o, rec_w_in, rec_conv_w, rec_conv_b, rec_gate_w, rec_gate_b, rec_a_param, rec_w_out, moe_w_group, moe_b_group, moe_w_expert, moe_b_expert, moe_w13, moe_w2):
    xt = x.reshape(TOKENS, D_MODEL)
    mod_all = _ada_mod(c, ada_w, ada_b).reshape(DEPTH, BATCH, 6, D_MODEL)

    pad = ROUTER_LANES - N_GROUPS - N_EXPERTS
    w_router = jnp.concatenate(
        [moe_w_group, moe_w_expert, jnp.zeros((DEPTH, D_MODEL, pad), F32)], axis=-1)
    b_router = jnp.concatenate(
        [moe_b_group, moe_b_expert, jnp.zeros((DEPTH, pad), F32)], axis=-1)
    wr_hi, wr_lo = _split_bf16(w_router)

    w_qkv = attn_w_qkv.astype(BF16)
    w_o = attn_w_o.astype(BF16)
    w_in = rec_w_in.astype(BF16)
    w_gate = rec_gate_w.astype(BF16)
    w_out = rec_w_out.astype(BF16)
    w13 = moe_w13.astype(BF16)
    w2 = moe_w2.astype(BF16)

    for layer in range(DEPTH):
        mod = mod_all[layer]
        j = layer // N_MIXERS
        if layer % N_MIXERS == 0:
            qkv = _norm_proj(xt, norm_mix[layer], mod, w_qkv[j], shift_row=0, scale_row=1)
            o = _diff_attention(qkv, attn_lambda[j], attn_subln[j], _lambda_init(layer))
            xt = _proj_residual(o, w_o[j], xt, mod, gate_row=2)
        else:
            xt = _recurrent_layer(xt, norm_mix[layer], mod, w_in[j], rec_conv_w[j], rec_conv_b[j],
                                  w_gate[j], rec_gate_b[j], rec_a_param[j], w_out[j])
        hf, comb = _router(xt, norm_ffn[layer], mod, wr_hi[layer], wr_lo[layer],
                           b_router[layer].reshape(1, ROUTER_LANES))
        xt = _moe_dense(hf, comb, w13[layer], w2[layer], xt, mod)

    return _final_norm(xt, final_norm).reshape(BATCH, SEQ, D_MODEL)
```

```python
import functools
import math

import jax
import jax.numpy as jnp
from jax import lax
from jax.experimental import pallas as pl
from jax.experimental.pallas import tpu as pltpu

D_MODEL = 1024
BATCH = 32
SEQ = 2048
TOKENS = BATCH * SEQ
DEPTH = 4
N_MIXERS = 2
HEAD_DIM = 64
ATTN_HEADS = D_MODEL // (2 * HEAD_DIM)
ATTN_WIDTH = ATTN_HEADS * 2 * HEAD_DIM
SUBLN_EPS = 1e-5
D_RNN = D_MODEL
RNN_HEADS = 4
RNN_BLOCK = D_RNN // RNN_HEADS
CONV_WIDTH = 4
RG_LRU_C = 8.0
N_GROUPS = 4
EXPERTS_PER_GROUP = 4
N_EXPERTS = N_GROUPS * EXPERTS_PER_GROUP
D_EXPERT = 512
NORM_EPS = 1e-6
Q_PRESCALE = HEAD_DIM ** -0.5 * math.log2(math.e)

LANES = 128
SUBLANES = 8
ROUTER_LANES = LANES
VMEM_LIMIT = 56 * 1024 * 1024

F32 = jnp.float32
BF16 = jnp.bfloat16


def _params(semantics):
    return pltpu.CompilerParams(dimension_semantics=semantics, vmem_limit_bytes=VMEM_LIMIT)


def _rms_mod(x, g, shift, scale, eps=NORM_EPS):
    y = x * lax.rsqrt(jnp.mean(x * x, axis=-1, keepdims=True) + eps) * g
    return y * (1.0 + scale) + shift


def _ada_kernel(c_ref, w_ref, b_ref, o_ref):
    c = c_ref[...]
    cond = (c * jax.nn.sigmoid(c)).astype(BF16)
    o_ref[...] = jnp.dot(cond, w_ref[...].astype(BF16), preferred_element_type=F32) + b_ref[...]


def _ada_mod(c, ada_w, ada_b):
    tn = 1536
    n = 6 * D_MODEL
    return pl.pallas_call(
        _ada_kernel,
        out_shape=jax.ShapeDtypeStruct((DEPTH, BATCH, n), F32),
        grid=(DEPTH, n // tn),
        in_specs=[
            pl.BlockSpec((BATCH, D_MODEL), lambda l, j: (0, 0)),
            pl.BlockSpec((None, D_MODEL, tn), lambda l, j: (l, 0, j)),
            pl.BlockSpec((None, 1, tn), lambda l, j: (l, 0, j)),
        ],
        out_specs=pl.BlockSpec((None, BATCH, tn), lambda l, j: (l, 0, j)),
        compiler_params=_params(("arbitrary", "arbitrary")),
        name="ada_mod",
    )(c, ada_w, ada_b.reshape(DEPTH, 1, n))


def _norm_proj_kernel(x_ref, g_ref, mod_ref, w_ref, o_ref, *, shift_row, scale_row, tn, out_scales):
    h = _rms_mod(x_ref[...], g_ref[...], mod_ref[shift_row:shift_row + 1, :],
                 mod_ref[scale_row:scale_row + 1, :]).astype(BF16)
    for j in range(o_ref.shape[1] // tn):
        y = jnp.dot(h, w_ref[:, j * tn:(j + 1) * tn], preferred_element_type=F32)
        if out_scales[j] != 1.0:
            y = y * out_scales[j]
        o_ref[:, j * tn:(j + 1) * tn] = y.astype(o_ref.dtype)


def _norm_proj(x, g, mod, w, *, shift_row, scale_row, out_scales, tm=512, tn=1024):
    n = w.shape[1]
    assert len(out_scales) == n // tn
    return pl.pallas_call(
        functools.partial(_norm_proj_kernel, shift_row=shift_row, scale_row=scale_row, tn=tn,
                          out_scales=out_scales),
        out_shape=jax.ShapeDtypeStruct((TOKENS, n), BF16),
        grid=(TOKENS // tm,),
        in_specs=[
            pl.BlockSpec((tm, D_MODEL), lambda i: (i, 0)),
            pl.BlockSpec((1, D_MODEL), lambda i: (0, 0)),
            pl.BlockSpec((None, 6, D_MODEL), lambda i: (i // (SEQ // tm), 0, 0)),
            pl.BlockSpec((D_MODEL, n), lambda i: (0, 0)),
        ],
        out_specs=pl.BlockSpec((tm, n), lambda i: (i, 0)),
        compiler_params=_params(("arbitrary",)),
        name="norm_proj",
    )(x, g.reshape(1, D_MODEL), mod, w)


def _attn_kernel(lam_ref, sg_ref, q_ref, k_ref, v_ref, o_ref, qm_scr, m_scr, l_scr, acc_scr,
                 *, tq, lambda_init):
    qi = pl.program_id(1)
    hw = 2 * HEAD_DIM
    lv = lam_ref[...]
    lam = (jnp.exp(jnp.sum(lv[0:1] * lv[1:2], keepdims=True))
           - jnp.exp(jnp.sum(lv[2:3] * lv[3:4], keepdims=True)) + lambda_init)

    lane = lax.broadcasted_iota(jnp.int32, (tq, hw), 1)
    for h in range(ATTN_HEADS):
        q = q_ref[:, h * hw:(h + 1) * hw]
        qm_scr[2 * h] = jnp.where(lane < HEAD_DIM, q, jnp.zeros_like(q))
        qm_scr[2 * h + 1] = jnp.where(lane >= HEAD_DIM, q, jnp.zeros_like(q))

    m_scr[...] = jnp.full(m_scr.shape, -jnp.inf, F32)
    l_scr[...] = jnp.zeros(l_scr.shape, F32)
    acc_scr[...] = jnp.zeros(acc_scr.shape, F32)

    def step(j, masked):
        start = pl.multiple_of(j * tq, tq)
        if masked:
            row = lax.broadcasted_iota(jnp.int32, (tq, tq), 0)
            col = lax.broadcasted_iota(jnp.int32, (tq, tq), 1)
            keep = row >= col
        for h in range(ATTN_HEADS):
            kb = k_ref[pl.ds(start, tq), h * hw:(h + 1) * hw]
            vb = v_ref[pl.ds(start, tq), h * hw:(h + 1) * hw]
            for mi in range(2):
                c = 2 * h + mi
                s = lax.dot_general(qm_scr[c], kb, (((1,), (1,)), ((), ())),
                                    preferred_element_type=F32)
                if masked:
                    s = jnp.where(keep, s, -jnp.inf)
                m_prev = m_scr[c]
                m_new = jnp.maximum(m_prev, jnp.max(s, axis=-1, keepdims=True))
                alpha = jnp.exp2(m_prev - m_new)
                p = jnp.exp2(s - jnp.concatenate([m_new] * (tq // LANES), axis=1))
                l_scr[c] = alpha * l_scr[c] + jnp.sum(p, axis=-1, keepdims=True)
                acc_scr[c] = alpha * acc_scr[c] + jnp.dot(
                    p.astype(BF16), vb, preferred_element_type=F32)
                m_scr[c] = m_new

    def body(j, carry):
        step(j, False)
        return carry

    lax.fori_loop(0, qi, body, 0)
    step(qi, True)

    for h in range(ATTN_HEADS):
        o = acc_scr[2 * h] / l_scr[2 * h] - lam * (acc_scr[2 * h + 1] / l_scr[2 * h + 1])
        o = o * lax.rsqrt(jnp.mean(o * o, axis=-1, keepdims=True) + SUBLN_EPS) * sg_ref[...]
        o_ref[:, h * hw:(h + 1) * hw] = (o * (1.0 - lambda_init)).astype(o_ref.dtype)


def _diff_attention(qkv, lam_vec, subln_g, lambda_init, *, tq=256):
    nq = SEQ // tq
    hw = 2 * HEAD_DIM
    assert hw == LANES
    chains = 2 * ATTN_HEADS
    return pl.pallas_call(
        functools.partial(_attn_kernel, tq=tq, lambda_init=lambda_init),
        out_shape=jax.ShapeDtypeStruct((TOKENS, ATTN_WIDTH), BF16),
        grid=(BATCH, nq),
        in_specs=[
            pl.BlockSpec((4, HEAD_DIM), lambda b, i: (0, 0)),
            pl.BlockSpec((1, hw), lambda b, i: (0, 0)),
            pl.BlockSpec((tq, ATTN_WIDTH), lambda b, i: (b * nq + i, 0)),
            pl.BlockSpec((SEQ, ATTN_WIDTH), lambda b, i: (b, 1)),
            pl.BlockSpec((SEQ, ATTN_WIDTH), lambda b, i: (b, 2)),
        ],
        out_specs=pl.BlockSpec((tq, ATTN_WIDTH), lambda b, i: (b * nq + i, 0)),
        scratch_shapes=[
            pltpu.VMEM((chains, tq, hw), BF16),
            pltpu.VMEM((chains, tq, LANES), F32),
            pltpu.VMEM((chains, tq, LANES), F32),
            pltpu.VMEM((chains, tq, hw), F32),
        ],
        compiler_params=_params(("arbitrary", "arbitrary")),
        name="diff_attention",
    )(lam_vec, subln_g.reshape(1, hw), qkv, qkv, qkv)


def _proj_residual_kernel(a_ref, w_ref, x_ref, mod_ref, o_ref, *, gate_row):
    y = jnp.dot(a_ref[...], w_ref[...], preferred_element_type=F32)
    o_ref[...] = x_ref[...] + mod_ref[gate_row:gate_row + 1, :] * y


def _proj_residual(a, w, x, mod, *, gate_row, tm=512):
    k = a.shape[1]
    return pl.pallas_call(
        functools.partial(_proj_residual_kernel, gate_row=gate_row),
        out_shape=jax.ShapeDtypeStruct((TOKENS, D_MODEL), F32),
        grid=(TOKENS // tm,),
        in_specs=[
            pl.BlockSpec((tm, k), lambda i: (i, 0)),
            pl.BlockSpec((k, D_MODEL), lambda i: (0, 0)),
            pl.BlockSpec((tm, D_MODEL), lambda i: (i, 0)),
            pl.BlockSpec((None, 6, D_MODEL), lambda i: (i // (SEQ // tm), 0, 0)),
        ],
        out_specs=pl.BlockSpec((tm, D_MODEL), lambda i: (i, 0)),
        compiler_params=_params(("arbitrary",)),
        name="proj_residual",
    )(a, w, x, mod)


def _gelu_tanh(y):
    c = math.sqrt(2.0 / math.pi)
    return 0.5 * y * (1.0 + jnp.tanh(c * (y + 0.044715 * (y * y * y))))


def _rec_kernel(x_ref, g_ref, mod_ref, w_in_ref, conv_w_ref, conv_b_ref, gate_w_ref, gate_b_ref,
                a_param_ref, w_out_ref, o_ref, ext_scr, a_scr, b_scr, h_scr, carry_scr, *, tm):
    t = pl.program_id(1)

    @pl.when(t == 0)
    def _():
        carry_scr[...] = jnp.zeros(carry_scr.shape, F32)
        ext_scr[0:SUBLANES, :] = jnp.zeros((SUBLANES, D_RNN), F32)

    x = x_ref[...]
    hm = _rms_mod(x, g_ref[...], mod_ref[0:1, :], mod_ref[1:2, :]).astype(BF16)
    y = jnp.dot(hm, w_in_ref[:, :D_RNN], preferred_element_type=F32)
    xr = jnp.dot(hm, w_in_ref[:, D_RNN:], preferred_element_type=F32)

    ext_scr[SUBLANES:SUBLANES + tm, :] = xr
    xc = conv_b_ref[...] + conv_w_ref[CONV_WIDTH - 1:CONV_WIDTH, :] * xr
    for k in range(CONV_WIDTH - 1):
        back = CONV_WIDTH - 1 - k
        xc = xc + conv_w_ref[k:k + 1, :] * ext_scr[SUBLANES - back:SUBLANES - back + tm, :]
    ext_scr[0:SUBLANES, :] = xr[tm - SUBLANES:, :]

    z = -a_param_ref[...]
    softplus = jnp.maximum(z, 0.0) + jnp.log(1.0 + jnp.exp(-jnp.abs(z)))
    xcb = xc.astype(BF16)
    for hd in range(RNN_HEADS):
        sl = slice(hd * RNN_BLOCK, (hd + 1) * RNN_BLOCK)
        gates = jnp.dot(xcb[:, sl], gate_w_ref[hd], preferred_element_type=F32) + gate_b_ref[hd]
        gates = jax.nn.sigmoid(gates)
        r = gates[:, :RNN_BLOCK]
        i = gates[:, RNN_BLOCK:]
        log_a = -RG_LRU_C * r * softplus[:, sl]
        a = jnp.exp(log_a)
        a_scr[:, sl] = a
        b_scr[:, sl] = jnp.sqrt(1.0 - a * a) * (i * xc[:, sl])

    sub = lax.broadcasted_iota(jnp.int32, (SUBLANES, D_RNN), 0)

    def scan_body(g, carry):
        r0 = pl.multiple_of(g * SUBLANES, SUBLANES)
        a = a_scr[pl.ds(r0, SUBLANES), :]
        b = b_scr[pl.ds(r0, SUBLANES), :]
        for d in (1, 2, 4):
            keep = sub >= d
            a_sh = jnp.where(keep, pltpu.roll(a, d, axis=0), 1.0)
            b_sh = jnp.where(keep, pltpu.roll(b, d, axis=0), 0.0)
            b = a * b_sh + b
            a = a * a_sh
        h = a * carry + b
        h_scr[pl.ds(r0, SUBLANES), :] = h
        return jnp.broadcast_to(h[SUBLANES - 1:SUBLANES, :], (SUBLANES, D_RNN))

    carry_scr[...] = lax.fori_loop(0, tm // SUBLANES, scan_body, carry_scr[...])

    mixed = (_gelu_tanh(y) * h_scr[...]).astype(BF16)
    out = jnp.dot(mixed, w_out_ref[...], preferred_element_type=F32)
    o_ref[...] = x + mod_ref[2:3, :] * out


def _recurrent_layer(x, g, mod, w_in, conv_w, conv_b, gate_w, gate_b, a_param, w_out, *, tm=512):
    nt = SEQ // tm
    const2 = lambda b, t: (0, 0)
    const3 = lambda b, t: (0, 0, 0)
    return pl.pallas_call(
        functools.partial(_rec_kernel, tm=tm),
        out_shape=jax.ShapeDtypeStruct((TOKENS, D_MODEL), F32),
        grid=(BATCH, nt),
        in_specs=[
            pl.BlockSpec((tm, D_MODEL), lambda b, t: (b * nt + t, 0)),
            pl.BlockSpec((1, D_MODEL), const2),
            pl.BlockSpec((None, 6, D_MODEL), lambda b, t: (b, 0, 0)),
            pl.BlockSpec((D_MODEL, 2 * D_RNN), const2),
            pl.BlockSpec((CONV_WIDTH, D_RNN), const2),
            pl.BlockSpec((1, D_RNN), const2),
            pl.BlockSpec((RNN_HEADS, RNN_BLOCK, 2 * RNN_BLOCK), const3),
            pl.BlockSpec((RNN_HEADS, 1, 2 * RNN_BLOCK), const3),
            pl.BlockSpec((1, D_RNN), const2),
            pl.BlockSpec((D_RNN, D_MODEL), const2),
        ],
        out_specs=pl.BlockSpec((tm, D_MODEL), lambda b, t: (b * nt + t, 0)),
        scratch_shapes=[
            pltpu.VMEM((tm + SUBLANES, D_RNN), F32),
            pltpu.VMEM((tm, D_RNN), F32),
            pltpu.VMEM((tm, D_RNN), F32),
            pltpu.VMEM((tm, D_RNN), F32),
            pltpu.VMEM((SUBLANES, D_RNN), F32),
        ],
        compiler_params=_params(("arbitrary", "arbitrary")),
        name="recurrent_layer",
    )(x, g.reshape(1, D_MODEL), mod, w_in, conv_w, conv_b.reshape(1, D_RNN), gate_w,
      gate_b.reshape(RNN_HEADS, 1, 2 * RNN_BLOCK), a_param.reshape(1, D_RNN), w_out)


def _router_kernel(x_ref, g_ref, mod_ref, w_hi_ref, w_lo_ref, b_ref, hf_ref, comb_ref):
    hf = _rms_mod(x_ref[...], g_ref[...], mod_ref[3:4, :], mod_ref[4:5, :])
    hi = hf.astype(BF16)
    hf_ref[...] = hi
    lo = (hf - hi.astype(F32)).astype(BF16)
    logits = (jnp.dot(hi, w_hi_ref[...], preferred_element_type=F32)
              + jnp.dot(lo, w_hi_ref[...], preferred_element_type=F32)
              + jnp.dot(hi, w_lo_ref[...], preferred_element_type=F32)) + b_ref[...]

    lane = lax.broadcasted_iota(jnp.int32, logits.shape, 1)
    neg = -jnp.inf
    big = ROUTER_LANES

    def masked_max(mask):
        return jnp.max(jnp.where(mask, logits, neg), axis=-1, keepdims=True)

    def first_lane(mask):
        return jnp.min(jnp.where(mask, lane, big), axis=-1, keepdims=True)

    gmask = lane < N_GROUPS
    gmax = masked_max(gmask)
    g_idx = first_lane(gmask & (logits == gmax))
    g_w = 1.0 / jnp.sum(jnp.where(gmask, jnp.exp(logits - gmax), 0.0), axis=-1, keepdims=True)

    e_lo = N_GROUPS + g_idx * EXPERTS_PER_GROUP
    emask = (lane >= e_lo) & (lane < e_lo + EXPERTS_PER_GROUP)
    v1 = masked_max(emask)
    i1 = first_lane(emask & (logits == v1))
    emask2 = emask & (lane != i1)
    v2 = masked_max(emask2)
    i2 = first_lane(emask2 & (logits == v2))
    e2 = jnp.exp(v2 - v1)
    p1 = 1.0 / (1.0 + e2)
    p2 = e2 * p1
    comb_ref[...] = jnp.where(lane == i1, p1 * g_w, jnp.where(lane == i2, p2 * g_w, 0.0))


def _router(x, g, mod, w_hi, w_lo, bias, *, tm=512):
    return pl.pallas_call(
        _router_kernel,
        out_shape=(jax.ShapeDtypeStruct((TOKENS, D_MODEL), BF16),
                   jax.ShapeDtypeStruct((TOKENS, ROUTER_LANES), F32)),
        grid=(TOKENS // tm,),
        in_specs=[
            pl.BlockSpec((tm, D_MODEL), lambda i: (i, 0)),
            pl.BlockSpec((1, D_MODEL), lambda i: (0, 0)),
            pl.BlockSpec((None, 6, D_MODEL), lambda i: (i // (SEQ // tm), 0, 0)),
            pl.BlockSpec((D_MODEL, ROUTER_LANES), lambda i: (0, 0)),
            pl.BlockSpec((D_MODEL, ROUTER_LANES), lambda i: (0, 0)),
            pl.BlockSpec((1, ROUTER_LANES), lambda i: (0, 0)),
        ],
        out_specs=(pl.BlockSpec((tm, D_MODEL), lambda i: (i, 0)),
                   pl.BlockSpec((tm, ROUTER_LANES), lambda i: (i, 0))),
        compiler_params=_params(("arbitrary",)),
        name="moe_router",
    )(x, g.reshape(1, D_MODEL), mod, w_hi, w_lo, bias)


def _moe_dense_kernel(hf_ref, comb_ref, w13_ref, w2_ref, x_ref, mod_ref, o_ref, acc_scr):
    e = pl.program_id(1)

    @pl.when(e == 0)
    def _():
        acc_scr[...] = jnp.zeros(acc_scr.shape, F32)

    u = jnp.dot(hf_ref[...], w13_ref[...], preferred_element_type=F32)
    u1 = u[:, :D_EXPERT]
    act = (u1 * jax.nn.sigmoid(u1) * u[:, D_EXPERT:]).astype(BF16)
    y = jnp.dot(act, w2_ref[...], preferred_element_type=F32)
    comb = comb_ref[...]
    lane = lax.broadcasted_iota(jnp.int32, comb.shape, 1)
    cw = jnp.sum(jnp.where(lane == e + N_GROUPS, comb, 0.0), axis=-1, keepdims=True)
    acc_scr[...] += cw * y

    @pl.when(e == N_EXPERTS - 1)
    def _():
        o_ref[...] = x_ref[...] + mod_ref[5:6, :] * acc_scr[...]


def _moe_dense(hf, comb, w13, w2, x, mod, *, tm=1024):
    return pl.pallas_call(
        _moe_dense_kernel,
        out_shape=jax.ShapeDtypeStruct((TOKENS, D_MODEL), F32),
        grid=(TOKENS // tm, N_EXPERTS),
        in_specs=[
            pl.BlockSpec((tm, D_MODEL), lambda i, e: (i, 0)),
            pl.BlockSpec((tm, ROUTER_LANES), lambda i, e: (i, 0)),
            pl.BlockSpec((None, D_MODEL, 2 * D_EXPERT), lambda i, e: (e, 0, 0)),
            pl.BlockSpec((None, D_EXPERT, D_MODEL), lambda i, e: (e, 0, 0)),
            pl.BlockSpec((tm, D_MODEL), lambda i, e: (i, 0)),
            pl.BlockSpec((None, 6, D_MODEL), lambda i, e: (i // (SEQ // tm), 0, 0)),
        ],
        out_specs=pl.BlockSpec((tm, D_MODEL), lambda i, e: (i, 0)),
        scratch_shapes=[pltpu.VMEM((tm, D_MODEL), F32)],
        compiler_params=_params(("arbitrary", "arbitrary")),
        name="moe_dense",
    )(hf, comb, w13, w2, x, mod)


def _final_norm_kernel(x_ref, g_ref, o_ref):
    x = x_ref[...]
    o_ref[...] = x * lax.rsqrt(jnp.mean(x * x, axis=-1, keepdims=True) + NORM_EPS) * g_ref[...]


def _final_norm(x, g, *, tm=1024):
    return pl.pallas_call(
        _final_norm_kernel,
        out_shape=jax.ShapeDtypeStruct((TOKENS, D_MODEL), F32),
        grid=(TOKENS // tm,),
        in_specs=[pl.BlockSpec((tm, D_MODEL), lambda i: (i, 0)),
                  pl.BlockSpec((1, D_MODEL), lambda i: (0, 0))],
        out_specs=pl.BlockSpec((tm, D_MODEL), lambda i: (i, 0)),
        compiler_params=_params(("arbitrary",)),
        name="final_norm",
    )(x, g.reshape(1, D_MODEL))


def _lambda_init(layer):
    return 0.8 - 0.6 * math.exp(-0.3 * layer)


def _split_bf16(w):
    hi = w.astype(BF16)
    return hi, (w - hi.astype(F32)).astype(BF16)


def kernel(x, c, norm_mix, norm_ffn, final_norm, ada_w, ada_b, attn_w_qkv, attn_lambda, attn_subln, attn_w_o, rec_w_in, rec_conv_w, rec_conv_b, rec_gate_w, rec_gate_b, rec_a_param, rec_w_out, moe_w_group, moe_b_group, moe_w_expert, moe_b_expert, moe_w13, moe_w2):
    xt = x.reshape(TOKENS, D_MODEL)
    mod_all = _ada_mod(c, ada_w, ada_b).reshape(DEPTH, BATCH, 6, D_MODEL)

    pad = ROUTER_LANES - N_GROUPS - N_EXPERTS
    w_router = jnp.concatenate(
        [moe_w_group, moe_w_expert, jnp.zeros((DEPTH, D_MODEL, pad), F32)], axis=-1)
    b_router = jnp.concatenate(
        [moe_b_group, moe_b_expert, jnp.zeros((DEPTH, pad), F32)], axis=-1)
    wr_hi, wr_lo = _split_bf16(w_router)

    w_qkv = attn_w_qkv.astype(BF16)
    w_o = attn_w_o.astype(BF16)
    w_in = rec_w_in.astype(BF16)
    w_gate = rec_gate_w.astype(BF16)
    w_out = rec_w_out.astype(BF16)
    w13 = moe_w13.astype(BF16)
    w2 = moe_w2.astype(BF16)

    for layer in range(DEPTH):
        mod = mod_all[layer]
        j = layer // N_MIXERS
        if layer % N_MIXERS == 0:
            qkv = _norm_proj(xt, norm_mix[layer], mod, w_qkv[j], shift_row=0, scale_row=1,
                             out_scales=(Q_PRESCALE, 1.0, 1.0))
            o = _diff_attention(qkv, attn_lambda[j], attn_subln[j], _lambda_init(layer))
            xt = _proj_residual(o, w_o[j], xt, mod, gate_row=2)
        else:
            xt = _recurrent_layer(xt, norm_mix[layer], mod, w_in[j], rec_conv_w[j], rec_conv_b[j],
                                  w_gate[j], rec_gate_b[j], rec_a_param[j], w_out[j])
        hf, comb = _router(xt, norm_ffn[layer], mod, wr_hi[layer], wr_lo[layer],
                           b_router[layer].reshape(1, ROUTER_LANES))
        xt = _moe_dense(hf, comb, w13[layer], w2[layer], xt, mod)

    return _final_norm(xt, final_norm).reshape(BATCH, SEQ, D_MODEL)
```

```python
import functools
import math

import jax
import jax.numpy as jnp
from jax import lax
from jax.experimental import pallas as pl
from jax.experimental.pallas import tpu as pltpu

D_MODEL = 1024
BATCH = 32
SEQ = 2048
TOKENS = BATCH * SEQ
DEPTH = 4
N_MIXERS = 2
HEAD_DIM = 64
ATTN_HEADS = D_MODEL // (2 * HEAD_DIM)
ATTN_WIDTH = ATTN_HEADS * 2 * HEAD_DIM
SUBLN_EPS = 1e-5
D_RNN = D_MODEL
RNN_HEADS = 4
RNN_BLOCK = D_RNN // RNN_HEADS
CONV_WIDTH = 4
RG_LRU_C = 8.0
N_GROUPS = 4
EXPERTS_PER_GROUP = 4
N_EXPERTS = N_GROUPS * EXPERTS_PER_GROUP
TOP_K = 2
D_EXPERT = 512
NORM_EPS = 1e-6
Q_PRESCALE = HEAD_DIM ** -0.5 * math.log2(math.e)

LANES = 128
SUBLANES = 8
ROUTER_LANES = LANES
VMEM_LIMIT = 56 * 1024 * 1024

F32 = jnp.float32
BF16 = jnp.bfloat16


def _params(semantics):
    return pltpu.CompilerParams(dimension_semantics=semantics, vmem_limit_bytes=VMEM_LIMIT)


def _rms_mod(x, g, shift, scale, eps=NORM_EPS):
    y = x * lax.rsqrt(jnp.mean(x * x, axis=-1, keepdims=True) + eps) * g
    return y * (1.0 + scale) + shift


def _ada_kernel(c_ref, w_ref, b_ref, o_ref):
    c = c_ref[...]
    cond = (c * jax.nn.sigmoid(c)).astype(BF16)
    o_ref[...] = jnp.dot(cond, w_ref[...].astype(BF16), preferred_element_type=F32) + b_ref[...]


def _ada_mod(c, ada_w, ada_b):
    tn = 1536
    n = 6 * D_MODEL
    return pl.pallas_call(
        _ada_kernel,
        out_shape=jax.ShapeDtypeStruct((DEPTH, BATCH, n), F32),
        grid=(DEPTH, n // tn),
        in_specs=[
            pl.BlockSpec((BATCH, D_MODEL), lambda l, j: (0, 0)),
            pl.BlockSpec((None, D_MODEL, tn), lambda l, j: (l, 0, j)),
            pl.BlockSpec((None, 1, tn), lambda l, j: (l, 0, j)),
        ],
        out_specs=pl.BlockSpec((None, BATCH, tn), lambda l, j: (l, 0, j)),
        compiler_params=_params(("arbitrary", "arbitrary")),
        name="ada_mod",
    )(c, ada_w, ada_b.reshape(DEPTH, 1, n))


def _norm_proj_kernel(x_ref, g_ref, mod_ref, w_ref, o_ref, *, shift_row, scale_row, tn, out_scales):
    h = _rms_mod(x_ref[...], g_ref[...], mod_ref[shift_row:shift_row + 1, :],
                 mod_ref[scale_row:scale_row + 1, :]).astype(BF16)
    for j in range(o_ref.shape[1] // tn):
        y = jnp.dot(h, w_ref[:, j * tn:(j + 1) * tn], preferred_element_type=F32)
        if out_scales[j] != 1.0:
            y = y * out_scales[j]
        o_ref[:, j * tn:(j + 1) * tn] = y.astype(o_ref.dtype)


def _norm_proj(x, g, mod, w, *, shift_row, scale_row, out_scales, tm=512, tn=1024):
    n = w.shape[1]
    assert len(out_scales) == n // tn
    return pl.pallas_call(
        functools.partial(_norm_proj_kernel, shift_row=shift_row, scale_row=scale_row, tn=tn,
                          out_scales=out_scales),
        out_shape=jax.ShapeDtypeStruct((TOKENS, n), BF16),
        grid=(TOKENS // tm,),
        in_specs=[
            pl.BlockSpec((tm, D_MODEL), lambda i: (i, 0)),
            pl.BlockSpec((1, D_MODEL), lambda i: (0, 0)),
            pl.BlockSpec((None, 6, D_MODEL), lambda i: (i // (SEQ // tm), 0, 0)),
            pl.BlockSpec((D_MODEL, n), lambda i: (0, 0)),
        ],
        out_specs=pl.BlockSpec((tm, n), lambda i: (i, 0)),
        compiler_params=_params(("arbitrary",)),
        name="norm_proj",
    )(x, g.reshape(1, D_MODEL), mod, w)


def _attn_kernel(lam_ref, sg_ref, q_ref, k_ref, v_ref, o_ref, qm_scr, m_scr, l_scr, acc_scr,
                 *, tq, lambda_init):
    qi = pl.program_id(1)
    hw = 2 * HEAD_DIM
    lv = lam_ref[...]
    lam = (jnp.exp(jnp.sum(lv[0:1] * lv[1:2], keepdims=True))
           - jnp.exp(jnp.sum(lv[2:3] * lv[3:4], keepdims=True)) + lambda_init)

    lane = lax.broadcasted_iota(jnp.int32, (tq, hw), 1)
    for h in range(ATTN_HEADS):
        q = q_ref[:, h * hw:(h + 1) * hw]
        qm_scr[2 * h] = jnp.where(lane < HEAD_DIM, q, jnp.zeros_like(q))
        qm_scr[2 * h + 1] = jnp.where(lane >= HEAD_DIM, q, jnp.zeros_like(q))

    m_scr[...] = jnp.full(m_scr.shape, -jnp.inf, F32)
    l_scr[...] = jnp.zeros(l_scr.shape, F32)
    acc_scr[...] = jnp.zeros(acc_scr.shape, F32)

    def step(j, masked):
        start = pl.multiple_of(j * tq, tq)
        if masked:
            row = lax.broadcasted_iota(jnp.int32, (tq, tq), 0)
            col = lax.broadcasted_iota(jnp.int32, (tq, tq), 1)
            keep = row >= col
        for h in range(ATTN_HEADS):
            kb = k_ref[pl.ds(start, tq), h * hw:(h + 1) * hw]
            vb = v_ref[pl.ds(start, tq), h * hw:(h + 1) * hw]
            for mi in range(2):
                c = 2 * h + mi
                s = lax.dot_general(qm_scr[c], kb, (((1,), (1,)), ((), ())),
                                    preferred_element_type=F32)
                if masked:
                    s = jnp.where(keep, s, -jnp.inf)
                m_prev = m_scr[c]
                m_new = jnp.maximum(m_prev, jnp.max(s, axis=-1, keepdims=True))
                alpha = jnp.exp2(m_prev - m_new)
                p = jnp.exp2(s - jnp.concatenate([m_new] * (tq // LANES), axis=1))
                l_scr[c] = alpha * l_scr[c] + jnp.sum(p, axis=-1, keepdims=True)
                acc_scr[c] = alpha * acc_scr[c] + jnp.dot(
                    p.astype(BF16), vb, preferred_element_type=F32)
                m_scr[c] = m_new

    def body(j, carry):
        step(j, False)
        return carry

    lax.fori_loop(0, qi, body, 0)
    step(qi, True)

    for h in range(ATTN_HEADS):
        o = acc_scr[2 * h] / l_scr[2 * h] - lam * (acc_scr[2 * h + 1] / l_scr[2 * h + 1])
        o = o * lax.rsqrt(jnp.mean(o * o, axis=-1, keepdims=True) + SUBLN_EPS) * sg_ref[...]
        o_ref[:, h * hw:(h + 1) * hw] = (o * (1.0 - lambda_init)).astype(o_ref.dtype)


def _diff_attention(qkv, lam_vec, subln_g, lambda_init, *, tq=256):
    nq = SEQ // tq
    hw = 2 * HEAD_DIM
    assert hw == LANES
    chains = 2 * ATTN_HEADS
    return pl.pallas_call(
        functools.partial(_attn_kernel, tq=tq, lambda_init=lambda_init),
        out_shape=jax.ShapeDtypeStruct((TOKENS, ATTN_WIDTH), BF16),
        grid=(BATCH, nq),
        in_specs=[
            pl.BlockSpec((4, HEAD_DIM), lambda b, i: (0, 0)),
            pl.BlockSpec((1, hw), lambda b, i: (0, 0)),
            pl.BlockSpec((tq, ATTN_WIDTH), lambda b, i: (b * nq + i, 0)),
            pl.BlockSpec((SEQ, ATTN_WIDTH), lambda b, i: (b, 1)),
            pl.BlockSpec((SEQ, ATTN_WIDTH), lambda b, i: (b, 2)),
        ],
        out_specs=pl.BlockSpec((tq, ATTN_WIDTH), lambda b, i: (b * nq + i, 0)),
        scratch_shapes=[
            pltpu.VMEM((chains, tq, hw), BF16),
            pltpu.VMEM((chains, tq, LANES), F32),
            pltpu.VMEM((chains, tq, LANES), F32),
            pltpu.VMEM((chains, tq, hw), F32),
        ],
        compiler_params=_params(("arbitrary", "arbitrary")),
        name="diff_attention",
    )(lam_vec, subln_g.reshape(1, hw), qkv, qkv, qkv)


def _proj_residual_kernel(a_ref, w_ref, x_ref, mod_ref, o_ref, *, gate_row):
    y = jnp.dot(a_ref[...], w_ref[...], preferred_element_type=F32)
    o_ref[...] = x_ref[...] + mod_ref[gate_row:gate_row + 1, :] * y


def _proj_residual(a, w, x, mod, *, gate_row, tm=512):
    k = a.shape[1]
    return pl.pallas_call(
        functools.partial(_proj_residual_kernel, gate_row=gate_row),
        out_shape=jax.ShapeDtypeStruct((TOKENS, D_MODEL), F32),
        grid=(TOKENS // tm,),
        in_specs=[
            pl.BlockSpec((tm, k), lambda i: (i, 0)),
            pl.BlockSpec((k, D_MODEL), lambda i: (0, 0)),
            pl.BlockSpec((tm, D_MODEL), lambda i: (i, 0)),
            pl.BlockSpec((None, 6, D_MODEL), lambda i: (i // (SEQ // tm), 0, 0)),
        ],
        out_specs=pl.BlockSpec((tm, D_MODEL), lambda i: (i, 0)),
        compiler_params=_params(("arbitrary",)),
        name="proj_residual",
    )(a, w, x, mod)


def _gelu_tanh(y):
    c = math.sqrt(2.0 / math.pi)
    return 0.5 * y * (1.0 + jnp.tanh(c * (y + 0.044715 * (y * y * y))))


def _rec_kernel(x_ref, g_ref, mod_ref, w_in_ref, conv_w_ref, conv_b_ref, gate_w_ref, gate_b_ref,
                a_param_ref, w_out_ref, o_ref, ext_scr, a_scr, b_scr, h_scr, carry_scr, *, tm):
    t = pl.program_id(1)

    @pl.when(t == 0)
    def _():
        carry_scr[...] = jnp.zeros(carry_scr.shape, F32)
        ext_scr[0:SUBLANES, :] = jnp.zeros((SUBLANES, D_RNN), F32)

    x = x_ref[...]
    hm = _rms_mod(x, g_ref[...], mod_ref[0:1, :], mod_ref[1:2, :]).astype(BF16)
    y = jnp.dot(hm, w_in_ref[:, :D_RNN], preferred_element_type=F32)
    xr = jnp.dot(hm, w_in_ref[:, D_RNN:], preferred_element_type=F32)

    ext_scr[SUBLANES:SUBLANES + tm, :] = xr
    xc = conv_b_ref[...] + conv_w_ref[CONV_WIDTH - 1:CONV_WIDTH, :] * xr
    for k in range(CONV_WIDTH - 1):
        back = CONV_WIDTH - 1 - k
        xc = xc + conv_w_ref[k:k + 1, :] * ext_scr[SUBLANES - back:SUBLANES - back + tm, :]
    ext_scr[0:SUBLANES, :] = xr[tm - SUBLANES:, :]

    z = -a_param_ref[...]
    softplus = jnp.maximum(z, 0.0) + jnp.log(1.0 + jnp.exp(-jnp.abs(z)))
    xcb = xc.astype(BF16)
    for hd in range(RNN_HEADS):
        sl = slice(hd * RNN_BLOCK, (hd + 1) * RNN_BLOCK)
        gates = jnp.dot(xcb[:, sl], gate_w_ref[hd], preferred_element_type=F32) + gate_b_ref[hd]
        gates = jax.nn.sigmoid(gates)
        r = gates[:, :RNN_BLOCK]
        i = gates[:, RNN_BLOCK:]
        log_a = -RG_LRU_C * r * softplus[:, sl]
        a = jnp.exp(log_a)
        a_scr[:, sl] = a
        b_scr[:, sl] = jnp.sqrt(1.0 - a * a) * (i * xc[:, sl])

    sub = lax.broadcasted_iota(jnp.int32, (SUBLANES, D_RNN), 0)

    def scan_body(g, carry):
        r0 = pl.multiple_of(g * SUBLANES, SUBLANES)
        a = a_scr[pl.ds(r0, SUBLANES), :]
        b = b_scr[pl.ds(r0, SUBLANES), :]
        for d in (1, 2, 4):
            keep = sub >= d
            a_sh = jnp.where(keep, pltpu.roll(a, d, axis=0), 1.0)
            b_sh = jnp.where(keep, pltpu.roll(b, d, axis=0), 0.0)
            b = a * b_sh + b
            a = a * a_sh
        h = a * carry + b
        h_scr[pl.ds(r0, SUBLANES), :] = h
        return jnp.broadcast_to(h[SUBLANES - 1:SUBLANES, :], (SUBLANES, D_RNN))

    carry_scr[...] = lax.fori_loop(0, tm // SUBLANES, scan_body, carry_scr[...])

    mixed = (_gelu_tanh(y) * h_scr[...]).astype(BF16)
    out = jnp.dot(mixed, w_out_ref[...], preferred_element_type=F32)
    o_ref[...] = x + mod_ref[2:3, :] * out


def _recurrent_layer(x, g, mod, w_in, conv_w, conv_b, gate_w, gate_b, a_param, w_out, *, tm=512):
    nt = SEQ // tm
    const2 = lambda b, t: (0, 0)
    const3 = lambda b, t: (0, 0, 0)
    return pl.pallas_call(
        functools.partial(_rec_kernel, tm=tm),
        out_shape=jax.ShapeDtypeStruct((TOKENS, D_MODEL), F32),
        grid=(BATCH, nt),
        in_specs=[
            pl.BlockSpec((tm, D_MODEL), lambda b, t: (b * nt + t, 0)),
            pl.BlockSpec((1, D_MODEL), const2),
            pl.BlockSpec((None, 6, D_MODEL), lambda b, t: (b, 0, 0)),
            pl.BlockSpec((D_MODEL, 2 * D_RNN), const2),
            pl.BlockSpec((CONV_WIDTH, D_RNN), const2),
            pl.BlockSpec((1, D_RNN), const2),
            pl.BlockSpec((RNN_HEADS, RNN_BLOCK, 2 * RNN_BLOCK), const3),
            pl.BlockSpec((RNN_HEADS, 1, 2 * RNN_BLOCK), const3),
            pl.BlockSpec((1, D_RNN), const2),
            pl.BlockSpec((D_RNN, D_MODEL), const2),
        ],
        out_specs=pl.BlockSpec((tm, D_MODEL), lambda b, t: (b * nt + t, 0)),
        scratch_shapes=[
            pltpu.VMEM((tm + SUBLANES, D_RNN), F32),
            pltpu.VMEM((tm, D_RNN), F32),
            pltpu.VMEM((tm, D_RNN), F32),
            pltpu.VMEM((tm, D_RNN), F32),
            pltpu.VMEM((SUBLANES, D_RNN), F32),
        ],
        compiler_params=_params(("arbitrary", "arbitrary")),
        name="recurrent_layer",
    )(x, g.reshape(1, D_MODEL), mod, w_in, conv_w, conv_b.reshape(1, D_RNN), gate_w,
      gate_b.reshape(RNN_HEADS, 1, 2 * RNN_BLOCK), a_param.reshape(1, D_RNN), w_out)


MOE_BLOCK = SEQ
PACK_ROWS = D_MODEL // (2 * LANES)
ROW_GRAN = 128
GRAN_SHIFT = ROW_GRAN.bit_length() - 1
TILE_STRIDE = ROW_GRAN + SUBLANES
TILE_ROWS = PACK_ROWS * TILE_STRIDE
N_TILES = TOP_K * MOE_BLOCK // ROW_GRAN + N_EXPERTS
SUPER_ROWS = 512
SUPER_SHIFT = SUPER_ROWS.bit_length() - 1
EPI_ROWS = 256
N_EPI = MOE_BLOCK // EPI_ROWS
ROUTER_TM = 512
GROUP_ROW0 = 0
EXPERT_ROW0 = SUBLANES
HI_MASK = 0xFFFF0000


def _pack_words(lo, hi):
    lo_bits = lax.bitcast_convert_type(lo.astype(BF16).astype(F32), jnp.uint32)
    hi_bits = lax.bitcast_convert_type(hi.astype(BF16).astype(F32), jnp.uint32)
    return (hi_bits & jnp.uint32(HI_MASK)) | (lo_bits >> jnp.uint32(16))


def _unpack_words(w):
    lo = lax.bitcast_convert_type(w << jnp.uint32(16), F32)
    hi = lax.bitcast_convert_type(w & jnp.uint32(HI_MASK), F32)
    return lo, hi


def _router_kernel(x_ref, g_ref, mod_ref, w_hi_ref, w_lo_ref, b_ref, hfp_ref, ids_ref, wts_ref):
    tm = x_ref.shape[0]
    hf = _rms_mod(x_ref[...], g_ref[...], mod_ref[3:4, :], mod_ref[4:5, :])
    hi = hf.astype(BF16)
    hi32 = hi.astype(F32)
    half = D_MODEL // 2
    for j in range(PACK_ROWS):
        lo_bits = lax.bitcast_convert_type(hi32[:, j * LANES:(j + 1) * LANES], jnp.uint32)
        hi_bits = lax.bitcast_convert_type(hi32[:, half + j * LANES:half + (j + 1) * LANES],
                                           jnp.uint32)
        words = (hi_bits & jnp.uint32(HI_MASK)) | (lo_bits >> jnp.uint32(16))
        hfp_ref[pl.ds(j, tm, stride=PACK_ROWS), :] = words

    lo = (hf - hi32).astype(BF16)
    logits = (jnp.dot(hi, w_hi_ref[...], preferred_element_type=F32)
              + jnp.dot(lo, w_hi_ref[...], preferred_element_type=F32)
              + jnp.dot(hi, w_lo_ref[...], preferred_element_type=F32)) + b_ref[...]
    lt = logits.T

    neg = -jnp.inf
    row8 = lax.broadcasted_iota(jnp.int32, (SUBLANES, tm), 0)
    gl = jnp.where(row8 < N_GROUPS, lt[GROUP_ROW0:GROUP_ROW0 + SUBLANES, :], neg)
    gmax = jnp.max(gl, axis=0, keepdims=True)
    g_idx = jnp.min(jnp.where(gl == gmax, row8, SUBLANES), axis=0, keepdims=True)
    g_w = 1.0 / jnp.sum(jnp.exp(gl - gmax), axis=0, keepdims=True)

    el = lt[EXPERT_ROW0:EXPERT_ROW0 + N_EXPERTS, :]
    row16 = lax.broadcasted_iota(jnp.int32, (N_EXPERTS, tm), 0)
    emask = (row16 >> (EXPERTS_PER_GROUP.bit_length() - 1)) == g_idx
    v1 = jnp.max(jnp.where(emask, el, neg), axis=0, keepdims=True)
    i1 = jnp.min(jnp.where(emask & (el == v1), row16, N_EXPERTS), axis=0, keepdims=True)
    emask2 = emask & (row16 != i1)
    v2 = jnp.max(jnp.where(emask2, el, neg), axis=0, keepdims=True)
    i2 = jnp.min(jnp.where(emask2 & (el == v2), row16, N_EXPERTS), axis=0, keepdims=True)
    e2 = jnp.exp(v2 - v1)
    p1 = 1.0 / (1.0 + e2)
    p2 = e2 * p1
    ids_ref[...] = jnp.where(row8 == 0, i1, jnp.where(row8 == 1, i2, 0))
    wts_ref[...] = jnp.where(row8 == 0, p1 * g_w, jnp.where(row8 == 1, p2 * g_w, 0.0))


def _router(x, g, mod, w_hi, w_lo, bias):
    tm = ROUTER_TM
    nt = TOKENS // tm
    return pl.pallas_call(
        _router_kernel,
        out_shape=(jax.ShapeDtypeStruct((TOKENS * PACK_ROWS, LANES), jnp.uint32),
                   jax.ShapeDtypeStruct((nt, SUBLANES, tm), jnp.int32),
                   jax.ShapeDtypeStruct((nt, SUBLANES, tm), F32)),
        grid=(nt,),
        in_specs=[
            pl.BlockSpec((tm, D_MODEL), lambda i: (i, 0)),
            pl.BlockSpec((1, D_MODEL), lambda i: (0, 0)),
            pl.BlockSpec((None, 6, D_MODEL), lambda i: (i // (SEQ // tm), 0, 0)),
            pl.BlockSpec((D_MODEL, ROUTER_LANES), lambda i: (0, 0)),
            pl.BlockSpec((D_MODEL, ROUTER_LANES), lambda i: (0, 0)),
            pl.BlockSpec((1, ROUTER_LANES), lambda i: (0, 0)),
        ],
        out_specs=(pl.BlockSpec((tm * PACK_ROWS, LANES), lambda i: (i, 0)),
                   pl.BlockSpec((None, SUBLANES, tm), lambda i: (i, 0, 0)),
                   pl.BlockSpec((None, SUBLANES, tm), lambda i: (i, 0, 0))),
        compiler_params=_params(("arbitrary",)),
        name="moe_router",
    )(x, g.reshape(1, D_MODEL), mod, w_hi, w_lo, bias)


def _plan_kernel(ids_ref, wts_ref, addr_ref, gw_ref, meta_ref, rank_scr):
    nc, _, tm = ids_ref.shape
    row = lax.broadcasted_iota(jnp.int32, (N_EXPERTS, tm), 0)
    upper = (lax.broadcasted_iota(jnp.int32, (tm, tm), 0)
             < lax.broadcasted_iota(jnp.int32, (tm, tm), 1)).astype(BF16)

    def member(c):
        e0 = ids_ref[c, 0:1, :]
        e1 = ids_ref[c, 1:2, :]
        return e0, e1, ((row == e0) | (row == e1))

    counts = jnp.zeros((N_EXPERTS, 1), F32)
    for c in range(nc):
        _, _, mem = member(c)
        memf = mem.astype(F32)
        rank_scr[c] = jnp.dot(memf.astype(BF16), upper, preferred_element_type=F32) + counts
        counts = counts + jnp.sum(memf, axis=1, keepdims=True)

    tiles = jnp.floor((counts + (ROW_GRAN - 1)) * (1.0 / ROW_GRAN))
    lower = (lax.broadcasted_iota(jnp.int32, (N_EXPERTS, N_EXPERTS), 1)
             < lax.broadcasted_iota(jnp.int32, (N_EXPERTS, N_EXPERTS), 0)).astype(BF16)
    tiles_b = jnp.broadcast_to(tiles, (N_EXPERTS, LANES))
    row_start = ROW_GRAN * jnp.dot(lower, tiles_b.astype(BF16), preferred_element_type=F32)
    meta_ref[0] = row_start.astype(jnp.int32)
    meta_ref[1] = (ROW_GRAN * tiles_b).astype(jnp.int32)

    rs = row_start[:, 0:1]
    for c in range(nc):
        e0, e1, _ = member(c)
        where = rs + rank_scr[c]
        for k, ek in enumerate((e0, e1)):
            pos = jnp.sum(jnp.where(row == ek, where, 0.0), axis=0, keepdims=True).astype(jnp.int32)
            addr = (pos >> GRAN_SHIFT) * TILE_ROWS + (pos & (ROW_GRAN - 1))
            addr_ref[k:k + 1, c * tm:(c + 1) * tm] = addr
            gw_ref[k:k + 1, c * tm:(c + 1) * tm] = wts_ref[c, k:k + 1, :]


def _plan(ids, wts):
    tm = ROUTER_TM
    nc = MOE_BLOCK // tm
    nb = TOKENS // MOE_BLOCK
    return pl.pallas_call(
        _plan_kernel,
        out_shape=(jax.ShapeDtypeStruct((nb, TOP_K, MOE_BLOCK), jnp.int32),
                   jax.ShapeDtypeStruct((nb, TOP_K, MOE_BLOCK), F32),
                   jax.ShapeDtypeStruct((nb, 2, N_EXPERTS, LANES), jnp.int32)),
        grid=(nb,),
        in_specs=[pl.BlockSpec((nc, SUBLANES, tm), lambda b: (b, 0, 0)),
                  pl.BlockSpec((nc, SUBLANES, tm), lambda b: (b, 0, 0))],
        out_specs=(pl.BlockSpec((None, TOP_K, MOE_BLOCK), lambda b: (b, 0, 0)),
                   pl.BlockSpec((None, TOP_K, MOE_BLOCK), lambda b: (b, 0, 0)),
                   pl.BlockSpec((None, 2, N_EXPERTS, LANES), lambda b: (b, 0, 0, 0))),
        scratch_shapes=[pltpu.VMEM((nc, N_EXPERTS, tm), F32)],
        compiler_params=_params(("arbitrary",)),
        name="moe_plan",
    )(ids, wts)


def _expert_rows(xy_ref, w13_ref, w2_ref, off, n_tiles):
    tile0 = lax.shift_right_logical(off, GRAN_SHIFT)
    bases = [pl.multiple_of((tile0 + ti) * TILE_ROWS, SUBLANES) for ti in range(n_tiles)]
    lo_cols, hi_cols = [], []
    for j in range(PACK_ROWS):
        words = jnp.concatenate(
            [xy_ref[pl.ds(b + j * TILE_STRIDE, ROW_GRAN), :] for b in bases], axis=0)
        lo, hi = _unpack_words(words)
        lo_cols.append(lo.astype(BF16))
        hi_cols.append(hi.astype(BF16))
    xin = jnp.concatenate(lo_cols + hi_cols, axis=1)
    u = jnp.dot(xin, w13_ref[...], preferred_element_type=F32)
    u1 = u[:, :D_EXPERT]
    act = (u1 * jax.nn.sigmoid(u1) * u[:, D_EXPERT:]).astype(BF16)
    y = jnp.dot(act, w2_ref[...], preferred_element_type=F32)
    half = D_MODEL // 2
    for j in range(PACK_ROWS):
        words = _pack_words(y[:, j * LANES:(j + 1) * LANES],
                            y[:, half + j * LANES:half + (j + 1) * LANES])
        for ti, b in enumerate(bases):
            xy_ref[pl.ds(b + j * TILE_STRIDE, ROW_GRAN), :] = words[ti * ROW_GRAN:(ti + 1) * ROW_GRAN]


def _moe_kernel(rs_ref, pad_ref, hfp_ref, addr_ref, gw_ref, w13_ref, w2_ref, x_ref, mod_ref,
                o_ref, xy_scr, out8_scr):
    b = pl.program_id(0)
    s = pl.program_id(1)
    unroll = SUBLANES

    @pl.when((b == 0) & (s == 0))
    def _():
        xy_scr[...] = jnp.zeros(xy_scr.shape, jnp.uint32)

    @pl.when(s == 0)
    def _():
        def group(g, carry):
            for i in range(unroll):
                t = g * unroll + i
                slab = hfp_ref[pl.ds(pl.multiple_of(g * (unroll * PACK_ROWS), unroll * PACK_ROWS)
                                     + i * PACK_ROWS, PACK_ROWS), :]
                for k in range(TOP_K):
                    xy_scr[pl.ds(addr_ref[k, t], PACK_ROWS, stride=TILE_STRIDE), :] = slab
            return carry
        lax.fori_loop(0, MOE_BLOCK // unroll, group, 0)

    @pl.when(s < N_EXPERTS)
    def _():
        e = b * N_EXPERTS + s
        row0 = rs_ref[e]
        padded = pad_ref[e]

        def super_tile(i, carry):
            off = row0 + i * SUPER_ROWS
            n_tiles = lax.shift_right_logical(
                jnp.minimum(padded - i * SUPER_ROWS, SUPER_ROWS), GRAN_SHIFT)
            for m in range(1, SUPER_ROWS // ROW_GRAN + 1):
                @pl.when(n_tiles == m)
                def _(m=m):
                    _expert_rows(xy_scr, w13_ref, w2_ref, off, m)
            return carry
        lax.fori_loop(0, lax.shift_right_logical(padded + (SUPER_ROWS - 1), SUPER_SHIFT), super_tile, 0)

    @pl.when(s >= N_EXPERTS)
    def _():
        t0 = (s - N_EXPERTS) * EPI_ROWS

        def group(g, carry):
            for i in range(unroll):
                tl = g * unroll + i
                lo_acc = hi_acc = None
                for k in range(TOP_K):
                    words = xy_scr[pl.ds(addr_ref[k, t0 + tl], PACK_ROWS, stride=TILE_STRIDE), :]
                    lo, hi = _unpack_words(words)
                    wk = gw_ref[k, t0 + tl]
                    lo_acc = wk * lo if lo_acc is None else lo_acc + wk * lo
                    hi_acc = wk * hi if hi_acc is None else hi_acc + wk * hi
                base = pl.multiple_of(g * (unroll * SUBLANES), unroll * SUBLANES) + i * SUBLANES
                out8_scr[pl.ds(base, PACK_ROWS), :] = lo_acc
                out8_scr[pl.ds(base + PACK_ROWS, PACK_ROWS), :] = hi_acc
            return carry
        lax.fori_loop(0, EPI_ROWS // unroll, group, 0)

        n_col = D_MODEL // LANES
        for g in range(EPI_ROWS // SUBLANES):
            for j in range(n_col):
                blk = out8_scr[pl.ds(g * SUBLANES * n_col + j, SUBLANES, stride=n_col), :]
                rows = slice(g * SUBLANES, (g + 1) * SUBLANES)
                cols = slice(j * LANES, (j + 1) * LANES)
                o_ref[rows, cols] = x_ref[rows, cols] + mod_ref[5:6, cols] * blk


def _moe_sparse(hfp, addr, gw, row_start, padded, w13, w2, x, mod):
    nb = TOKENS // MOE_BLOCK
    steps = N_EXPERTS + N_EPI

    def epi_block(b, s, *_):
        return (b * N_EPI + jnp.maximum(s - N_EXPERTS, 0), 0)

    def expert_block(b, s, *_):
        return (jnp.minimum(s, N_EXPERTS - 1), 0, 0)

    grid_spec = pltpu.PrefetchScalarGridSpec(
        num_scalar_prefetch=2,
        grid=(nb, steps),
        in_specs=[
            pl.BlockSpec((MOE_BLOCK * PACK_ROWS, LANES), lambda b, s, *_: (b, 0)),
            pl.BlockSpec((None, TOP_K, MOE_BLOCK), lambda b, s, *_: (b, 0, 0),
                         memory_space=pltpu.SMEM),
            pl.BlockSpec((None, TOP_K, MOE_BLOCK), lambda b, s, *_: (b, 0, 0),
                         memory_space=pltpu.SMEM),
            pl.BlockSpec((None, D_MODEL, 2 * D_EXPERT), expert_block),
            pl.BlockSpec((None, D_EXPERT, D_MODEL), expert_block),
            pl.BlockSpec((EPI_ROWS, D_MODEL), epi_block),
            pl.BlockSpec((None, 6, D_MODEL), lambda b, s, *_: (b, 0, 0)),
        ],
        out_specs=pl.BlockSpec((EPI_ROWS, D_MODEL), epi_block),
        scratch_shapes=[
            pltpu.VMEM((N_TILES * TILE_ROWS, LANES), jnp.uint32),
            pltpu.VMEM((EPI_ROWS * SUBLANES, LANES), F32),
        ],
    )
    return pl.pallas_call(
        _moe_kernel,
        out_shape=jax.ShapeDtypeStruct((TOKENS, D_MODEL), F32),
        grid_spec=grid_spec,
        compiler_params=_params(("arbitrary", "arbitrary")),
        name="moe_experts",
    )(row_start, padded, hfp, addr, gw, w13, w2, x, mod)


def _final_norm_kernel(x_ref, g_ref, o_ref):
    x = x_ref[...]
    o_ref[...] = x * lax.rsqrt(jnp.mean(x * x, axis=-1, keepdims=True) + NORM_EPS) * g_ref[...]


def _final_norm(x, g, *, tm=1024):
    return pl.pallas_call(
        _final_norm_kernel,
        out_shape=jax.ShapeDtypeStruct((TOKENS, D_MODEL), F32),
        grid=(TOKENS // tm,),
        in_specs=[pl.BlockSpec((tm, D_MODEL), lambda i: (i, 0)),
                  pl.BlockSpec((1, D_MODEL), lambda i: (0, 0))],
        out_specs=pl.BlockSpec((tm, D_MODEL), lambda i: (i, 0)),
        compiler_params=_params(("arbitrary",)),
        name="final_norm",
    )(x, g.reshape(1, D_MODEL))


def _lambda_init(layer):
    return 0.8 - 0.6 * math.exp(-0.3 * layer)


def _split_bf16(w):
    hi = w.astype(BF16)
    return hi, (w - hi.astype(F32)).astype(BF16)


def kernel(x, c, norm_mix, norm_ffn, final_norm, ada_w, ada_b, attn_w_qkv, attn_lambda, attn_subln, attn_w_o, rec_w_in, rec_conv_w, rec_conv_b, rec_gate_w, rec_gate_b, rec_a_param, rec_w_out, moe_w_group, moe_b_group, moe_w_expert, moe_b_expert, moe_w13, moe_w2):
    xt = x.reshape(TOKENS, D_MODEL)
    mod_all = _ada_mod(c, ada_w, ada_b).reshape(DEPTH, BATCH, 6, D_MODEL)

    def lane_pad(n, *lead):
        return jnp.zeros(lead + (n,), F32)

    gap = EXPERT_ROW0 - N_GROUPS
    tail = ROUTER_LANES - EXPERT_ROW0 - N_EXPERTS
    w_router = jnp.concatenate([moe_w_group, lane_pad(gap, DEPTH, D_MODEL), moe_w_expert,
                                lane_pad(tail, DEPTH, D_MODEL)], axis=-1)
    b_router = jnp.concatenate([moe_b_group, lane_pad(gap, DEPTH), moe_b_expert,
                                lane_pad(tail, DEPTH)], axis=-1)
    wr_hi, wr_lo = _split_bf16(w_router)

    w_qkv = attn_w_qkv.astype(BF16)
    w_o = attn_w_o.astype(BF16)
    w_in = rec_w_in.astype(BF16)
    w_gate = rec_gate_w.astype(BF16)
    w_out = rec_w_out.astype(BF16)
    w13 = moe_w13.astype(BF16)
    w2 = moe_w2.astype(BF16)

    for layer in range(DEPTH):
        mod = mod_all[layer]
        j = layer // N_MIXERS
        if layer % N_MIXERS == 0:
            qkv = _norm_proj(xt, norm_mix[layer], mod, w_qkv[j], shift_row=0, scale_row=1,
                             out_scales=(Q_PRESCALE, 1.0, 1.0))
            o = _diff_attention(qkv, attn_lambda[j], attn_subln[j], _lambda_init(layer))
            xt = _proj_residual(o, w_o[j], xt, mod, gate_row=2)
        else:
            xt = _recurrent_layer(xt, norm_mix[layer], mod, w_in[j], rec_conv_w[j], rec_conv_b[j],
                                  w_gate[j], rec_gate_b[j], rec_a_param[j], w_out[j])
        hfp, ids, wts = _router(xt, norm_ffn[layer], mod, wr_hi[layer], wr_lo[layer],
                                b_router[layer].reshape(1, ROUTER_LANES))
        addr, gw, meta = _plan(ids, wts)
        xt = _moe_sparse(hfp, addr, gw, meta[:, 0, :, 0].reshape(-1), meta[:, 1, :, 0].reshape(-1),
                         w13[layer], w2[layer], xt, mod)

    return _final_norm(xt, final_norm).reshape(BATCH, SEQ, D_MODEL)
```

```python
import functools
import math

import jax
import jax.numpy as jnp
from jax import lax
from jax.experimental import pallas as pl
from jax.experimental.pallas import tpu as pltpu

D_MODEL = 1024
BATCH = 32
SEQ = 2048
TOKENS = BATCH * SEQ
DEPTH = 4
N_MIXERS = 2
HEAD_DIM = 64
ATTN_HEADS = D_MODEL // (2 * HEAD_DIM)
ATTN_WIDTH = ATTN_HEADS * 2 * HEAD_DIM
SUBLN_EPS = 1e-5
D_RNN = D_MODEL
RNN_HEADS = 4
RNN_BLOCK = D_RNN // RNN_HEADS
CONV_WIDTH = 4
RG_LRU_C = 8.0
N_GROUPS = 4
EXPERTS_PER_GROUP = 4
N_EXPERTS = N_GROUPS * EXPERTS_PER_GROUP
TOP_K = 2
D_EXPERT = 512
NORM_EPS = 1e-6
Q_PRESCALE = HEAD_DIM ** -0.5 * math.log2(math.e)

LANES = 128
SUBLANES = 8
ROUTER_LANES = LANES
VMEM_LIMIT = 56 * 1024 * 1024

F32 = jnp.float32
BF16 = jnp.bfloat16


def _params(semantics):
    return pltpu.CompilerParams(dimension_semantics=semantics, vmem_limit_bytes=VMEM_LIMIT)


def _rms_mod(x, g, shift, scale, eps=NORM_EPS):
    y = x * lax.rsqrt(jnp.mean(x * x, axis=-1, keepdims=True) + eps) * g
    return y * (1.0 + scale) + shift


def _ada_kernel(c_ref, w_ref, b_ref, o_ref):
    c = c_ref[...]
    cond = (c * jax.nn.sigmoid(c)).astype(BF16)
    o_ref[...] = jnp.dot(cond, w_ref[...].astype(BF16), preferred_element_type=F32) + b_ref[...]


def _ada_mod(c, ada_w, ada_b):
    tn = 1536
    n = 6 * D_MODEL
    return pl.pallas_call(
        _ada_kernel,
        out_shape=jax.ShapeDtypeStruct((DEPTH, BATCH, n), F32),
        grid=(DEPTH, n // tn),
        in_specs=[
            pl.BlockSpec((BATCH, D_MODEL), lambda l, j: (0, 0)),
            pl.BlockSpec((None, D_MODEL, tn), lambda l, j: (l, 0, j)),
            pl.BlockSpec((None, 1, tn), lambda l, j: (l, 0, j)),
        ],
        out_specs=pl.BlockSpec((None, BATCH, tn), lambda l, j: (l, 0, j)),
        compiler_params=_params(("arbitrary", "arbitrary")),
        name="ada_mod",
    )(c, ada_w, ada_b.reshape(DEPTH, 1, n))


def _norm_proj_kernel(x_ref, g_ref, mod_ref, w_ref, o_ref, *, shift_row, scale_row, tn, out_scales):
    h = _rms_mod(x_ref[...], g_ref[...], mod_ref[shift_row:shift_row + 1, :],
                 mod_ref[scale_row:scale_row + 1, :]).astype(BF16)
    for j in range(o_ref.shape[1] // tn):
        y = jnp.dot(h, w_ref[:, j * tn:(j + 1) * tn], preferred_element_type=F32)
        if out_scales[j] != 1.0:
            y = y * out_scales[j]
        o_ref[:, j * tn:(j + 1) * tn] = y.astype(o_ref.dtype)


def _norm_proj(x, g, mod, w, *, shift_row, scale_row, out_scales, tm=512, tn=1024):
    n = w.shape[1]
    assert len(out_scales) == n // tn
    return pl.pallas_call(
        functools.partial(_norm_proj_kernel, shift_row=shift_row, scale_row=scale_row, tn=tn,
                          out_scales=out_scales),
        out_shape=jax.ShapeDtypeStruct((TOKENS, n), BF16),
        grid=(TOKENS // tm,),
        in_specs=[
            pl.BlockSpec((tm, D_MODEL), lambda i: (i, 0)),
            pl.BlockSpec((1, D_MODEL), lambda i: (0, 0)),
            pl.BlockSpec((None, 6, D_MODEL), lambda i: (i // (SEQ // tm), 0, 0)),
            pl.BlockSpec((D_MODEL, n), lambda i: (0, 0)),
        ],
        out_specs=pl.BlockSpec((tm, n), lambda i: (i, 0)),
        compiler_params=_params(("arbitrary",)),
        name="norm_proj",
    )(x, g.reshape(1, D_MODEL), mod, w)


def _attn_kernel(lam_ref, sg_ref, q_ref, k_ref, v_ref, o_ref, qm_scr, m_scr, l_scr, acc_scr,
                 *, tq, lambda_init):
    qi = pl.program_id(1)
    hw = 2 * HEAD_DIM
    lv = lam_ref[...]
    lam = (jnp.exp(jnp.sum(lv[0:1] * lv[1:2], keepdims=True))
           - jnp.exp(jnp.sum(lv[2:3] * lv[3:4], keepdims=True)) + lambda_init)

    lane = lax.broadcasted_iota(jnp.int32, (tq, hw), 1)
    for h in range(ATTN_HEADS):
        q = q_ref[:, h * hw:(h + 1) * hw]
        qm_scr[2 * h] = jnp.where(lane < HEAD_DIM, q, jnp.zeros_like(q))
        qm_scr[2 * h + 1] = jnp.where(lane >= HEAD_DIM, q, jnp.zeros_like(q))

    m_scr[...] = jnp.full(m_scr.shape, -jnp.inf, F32)
    l_scr[...] = jnp.zeros(l_scr.shape, F32)
    acc_scr[...] = jnp.zeros(acc_scr.shape, F32)

    def step(j, masked):
        start = pl.multiple_of(j * tq, tq)
        if masked:
            row = lax.broadcasted_iota(jnp.int32, (tq, tq), 0)
            col = lax.broadcasted_iota(jnp.int32, (tq, tq), 1)
            keep = row >= col
        for h in range(ATTN_HEADS):
            kb = k_ref[pl.ds(start, tq), h * hw:(h + 1) * hw]
            vb = v_ref[pl.ds(start, tq), h * hw:(h + 1) * hw]
            for mi in range(2):
                c = 2 * h + mi
                s = lax.dot_general(qm_scr[c], kb, (((1,), (1,)), ((), ())),
                                    preferred_element_type=F32)
                if masked:
                    s = jnp.where(keep, s, -jnp.inf)
                m_prev = m_scr[c]
                m_new = jnp.maximum(m_prev, jnp.max(s, axis=-1, keepdims=True))
                alpha = jnp.exp2(m_prev - m_new)
                p = jnp.exp2(s - jnp.concatenate([m_new] * (tq // LANES), axis=1))
                l_scr[c] = alpha * l_scr[c] + jnp.sum(p, axis=-1, keepdims=True)
                acc_scr[c] = alpha * acc_scr[c] + jnp.dot(
                    p.astype(BF16), vb, preferred_element_type=F32)
                m_scr[c] = m_new

    def body(j, carry):
        step(j, False)
        return carry

    lax.fori_loop(0, qi, body, 0)
    step(qi, True)

    for h in range(ATTN_HEADS):
        o = acc_scr[2 * h] / l_scr[2 * h] - lam * (acc_scr[2 * h + 1] / l_scr[2 * h + 1])
        o = o * lax.rsqrt(jnp.mean(o * o, axis=-1, keepdims=True) + SUBLN_EPS) * sg_ref[...]
        o_ref[:, h * hw:(h + 1) * hw] = (o * (1.0 - lambda_init)).astype(o_ref.dtype)


def _diff_attention(qkv, lam_vec, subln_g, lambda_init, *, tq=256):
    nq = SEQ // tq
    hw = 2 * HEAD_DIM
    assert hw == LANES
    chains = 2 * ATTN_HEADS
    return pl.pallas_call(
        functools.partial(_attn_kernel, tq=tq, lambda_init=lambda_init),
        out_shape=jax.ShapeDtypeStruct((TOKENS, ATTN_WIDTH), BF16),
        grid=(BATCH, nq),
        in_specs=[
            pl.BlockSpec((4, HEAD_DIM), lambda b, i: (0, 0)),
            pl.BlockSpec((1, hw), lambda b, i: (0, 0)),
            pl.BlockSpec((tq, ATTN_WIDTH), lambda b, i: (b * nq + i, 0)),
            pl.BlockSpec((SEQ, ATTN_WIDTH), lambda b, i: (b, 1)),
            pl.BlockSpec((SEQ, ATTN_WIDTH), lambda b, i: (b, 2)),
        ],
        out_specs=pl.BlockSpec((tq, ATTN_WIDTH), lambda b, i: (b * nq + i, 0)),
        scratch_shapes=[
            pltpu.VMEM((chains, tq, hw), BF16),
            pltpu.VMEM((chains, tq, LANES), F32),
            pltpu.VMEM((chains, tq, LANES), F32),
            pltpu.VMEM((chains, tq, hw), F32),
        ],
        compiler_params=_params(("arbitrary", "arbitrary")),
        name="diff_attention",
    )(lam_vec, subln_g.reshape(1, hw), qkv, qkv, qkv)


def _proj_residual_kernel(a_ref, w_ref, x_ref, mod_ref, o_ref, *, gate_row):
    y = jnp.dot(a_ref[...], w_ref[...], preferred_element_type=F32)
    o_ref[...] = x_ref[...] + mod_ref[gate_row:gate_row + 1, :] * y


def _proj_residual(a, w, x, mod, *, gate_row, tm=512):
    k = a.shape[1]
    return pl.pallas_call(
        functools.partial(_proj_residual_kernel, gate_row=gate_row),
        out_shape=jax.ShapeDtypeStruct((TOKENS, D_MODEL), F32),
        grid=(TOKENS // tm,),
        in_specs=[
            pl.BlockSpec((tm, k), lambda i: (i, 0)),
            pl.BlockSpec((k, D_MODEL), lambda i: (0, 0)),
            pl.BlockSpec((tm, D_MODEL), lambda i: (i, 0)),
            pl.BlockSpec((None, 6, D_MODEL), lambda i: (i // (SEQ // tm), 0, 0)),
        ],
        out_specs=pl.BlockSpec((tm, D_MODEL), lambda i: (i, 0)),
        compiler_params=_params(("arbitrary",)),
        name="proj_residual",
    )(a, w, x, mod)


def _gelu_tanh(y):
    c = math.sqrt(2.0 / math.pi)
    return 0.5 * y * (1.0 + jnp.tanh(c * (y + 0.044715 * (y * y * y))))


def _rec_kernel(x_ref, g_ref, mod_ref, w_in_ref, conv_w_ref, conv_b_ref, gate_w_ref, gate_b_ref,
                a_param_ref, w_out_ref, o_ref, ext_scr, a_scr, b_scr, h_scr, carry_scr, *, tm):
    t = pl.program_id(1)

    @pl.when(t == 0)
    def _():
        carry_scr[...] = jnp.zeros(carry_scr.shape, F32)
        ext_scr[0:SUBLANES, :] = jnp.zeros((SUBLANES, D_RNN), F32)

    x = x_ref[...]
    hm = _rms_mod(x, g_ref[...], mod_ref[0:1, :], mod_ref[1:2, :]).astype(BF16)
    y = jnp.dot(hm, w_in_ref[:, :D_RNN], preferred_element_type=F32)
    xr = jnp.dot(hm, w_in_ref[:, D_RNN:], preferred_element_type=F32)

    ext_scr[SUBLANES:SUBLANES + tm, :] = xr
    xc = conv_b_ref[...] + conv_w_ref[CONV_WIDTH - 1:CONV_WIDTH, :] * xr
    for k in range(CONV_WIDTH - 1):
        back = CONV_WIDTH - 1 - k
        xc = xc + conv_w_ref[k:k + 1, :] * ext_scr[SUBLANES - back:SUBLANES - back + tm, :]
    ext_scr[0:SUBLANES, :] = xr[tm - SUBLANES:, :]

    z = -a_param_ref[...]
    softplus = jnp.maximum(z, 0.0) + jnp.log(1.0 + jnp.exp(-jnp.abs(z)))
    xcb = xc.astype(BF16)
    for hd in range(RNN_HEADS):
        sl = slice(hd * RNN_BLOCK, (hd + 1) * RNN_BLOCK)
        gates = jnp.dot(xcb[:, sl], gate_w_ref[hd], preferred_element_type=F32) + gate_b_ref[hd]
        gates = jax.nn.sigmoid(gates)
        r = gates[:, :RNN_BLOCK]
        i = gates[:, RNN_BLOCK:]
        log_a = -RG_LRU_C * r * softplus[:, sl]
        a = jnp.exp(log_a)
        a_scr[:, sl] = a
        b_scr[:, sl] = jnp.sqrt(1.0 - a * a) * (i * xc[:, sl])

    sub = lax.broadcasted_iota(jnp.int32, (SUBLANES, D_RNN), 0)

    def scan_body(g, carry):
        r0 = pl.multiple_of(g * SUBLANES, SUBLANES)
        a = a_scr[pl.ds(r0, SUBLANES), :]
        b = b_scr[pl.ds(r0, SUBLANES), :]
        for d in (1, 2, 4):
            keep = sub >= d
            a_sh = jnp.where(keep, pltpu.roll(a, d, axis=0), 1.0)
            b_sh = jnp.where(keep, pltpu.roll(b, d, axis=0), 0.0)
            b = a * b_sh + b
            a = a * a_sh
        h = a * carry + b
        h_scr[pl.ds(r0, SUBLANES), :] = h
        return jnp.broadcast_to(h[SUBLANES - 1:SUBLANES, :], (SUBLANES, D_RNN))

    carry_scr[...] = lax.fori_loop(0, tm // SUBLANES, scan_body, carry_scr[...])

    mixed = (_gelu_tanh(y) * h_scr[...]).astype(BF16)
    out = jnp.dot(mixed, w_out_ref[...], preferred_element_type=F32)
    o_ref[...] = x + mod_ref[2:3, :] * out


def _recurrent_layer(x, g, mod, w_in, conv_w, conv_b, gate_w, gate_b, a_param, w_out, *, tm=512):
    nt = SEQ // tm
    const2 = lambda b, t: (0, 0)
    const3 = lambda b, t: (0, 0, 0)
    return pl.pallas_call(
        functools.partial(_rec_kernel, tm=tm),
        out_shape=jax.ShapeDtypeStruct((TOKENS, D_MODEL), F32),
        grid=(BATCH, nt),
        in_specs=[
            pl.BlockSpec((tm, D_MODEL), lambda b, t: (b * nt + t, 0)),
            pl.BlockSpec((1, D_MODEL), const2),
            pl.BlockSpec((None, 6, D_MODEL), lambda b, t: (b, 0, 0)),
            pl.BlockSpec((D_MODEL, 2 * D_RNN), const2),
            pl.BlockSpec((CONV_WIDTH, D_RNN), const2),
            pl.BlockSpec((1, D_RNN), const2),
            pl.BlockSpec((RNN_HEADS, RNN_BLOCK, 2 * RNN_BLOCK), const3),
            pl.BlockSpec((RNN_HEADS, 1, 2 * RNN_BLOCK), const3),
            pl.BlockSpec((1, D_RNN), const2),
            pl.BlockSpec((D_RNN, D_MODEL), const2),
        ],
        out_specs=pl.BlockSpec((tm, D_MODEL), lambda b, t: (b * nt + t, 0)),
        scratch_shapes=[
            pltpu.VMEM((tm + SUBLANES, D_RNN), F32),
            pltpu.VMEM((tm, D_RNN), F32),
            pltpu.VMEM((tm, D_RNN), F32),
            pltpu.VMEM((tm, D_RNN), F32),
            pltpu.VMEM((SUBLANES, D_RNN), F32),
        ],
        compiler_params=_params(("arbitrary", "arbitrary")),
        name="recurrent_layer",
    )(x, g.reshape(1, D_MODEL), mod, w_in, conv_w, conv_b.reshape(1, D_RNN), gate_w,
      gate_b.reshape(RNN_HEADS, 1, 2 * RNN_BLOCK), a_param.reshape(1, D_RNN), w_out)


MOE_BLOCK = SEQ
PACK_ROWS = D_MODEL // (2 * LANES)
ROW_GRAN = 64
GRAN_SHIFT = ROW_GRAN.bit_length() - 1
TILE_STRIDE = ROW_GRAN + SUBLANES
TILE_ROWS = PACK_ROWS * TILE_STRIDE
N_TILES = TOP_K * MOE_BLOCK // ROW_GRAN + N_EXPERTS
SUPER_ROWS = 512
SUPER_SHIFT = SUPER_ROWS.bit_length() - 1
EXPERTS_PER_STEP = 2
N_EXPERT_STEPS = N_EXPERTS // EXPERTS_PER_STEP
EPI_ROWS = 512
N_EPI = MOE_BLOCK // EPI_ROWS
ROUTER_TM = 512
GROUP_ROW0 = 0
EXPERT_ROW0 = SUBLANES
HI_MASK = 0xFFFF0000


def _pack_words(lo, hi):
    lo_bits = lax.bitcast_convert_type(lo.astype(BF16).astype(F32), jnp.uint32)
    hi_bits = lax.bitcast_convert_type(hi.astype(BF16).astype(F32), jnp.uint32)
    return (hi_bits & jnp.uint32(HI_MASK)) | (lo_bits >> jnp.uint32(16))


def _unpack_words(w):
    lo = lax.bitcast_convert_type(w << jnp.uint32(16), F32)
    hi = lax.bitcast_convert_type(w & jnp.uint32(HI_MASK), F32)
    return lo, hi


def _router_kernel(x_ref, g_ref, mod_ref, w_hi_ref, w_lo_ref, b_ref, hfp_ref, ids_ref, wts_ref):
    tm = x_ref.shape[0]
    hf = _rms_mod(x_ref[...], g_ref[...], mod_ref[3:4, :], mod_ref[4:5, :])
    hi = hf.astype(BF16)
    hi32 = hi.astype(F32)
    half = D_MODEL // 2
    for j in range(PACK_ROWS):
        lo_bits = lax.bitcast_convert_type(hi32[:, j * LANES:(j + 1) * LANES], jnp.uint32)
        hi_bits = lax.bitcast_convert_type(hi32[:, half + j * LANES:half + (j + 1) * LANES],
                                           jnp.uint32)
        words = (hi_bits & jnp.uint32(HI_MASK)) | (lo_bits >> jnp.uint32(16))
        hfp_ref[pl.ds(j, tm, stride=PACK_ROWS), :] = words

    lo = (hf - hi32).astype(BF16)
    logits = (jnp.dot(hi, w_hi_ref[...], preferred_element_type=F32)
              + jnp.dot(lo, w_hi_ref[...], preferred_element_type=F32)
              + jnp.dot(hi, w_lo_ref[...], preferred_element_type=F32)) + b_ref[...]
    lt = logits.T

    neg = -jnp.inf
    row8 = lax.broadcasted_iota(jnp.int32, (SUBLANES, tm), 0)
    gl = jnp.where(row8 < N_GROUPS, lt[GROUP_ROW0:GROUP_ROW0 + SUBLANES, :], neg)
    gmax = jnp.max(gl, axis=0, keepdims=True)
    g_idx = jnp.min(jnp.where(gl == gmax, row8, SUBLANES), axis=0, keepdims=True)
    g_w = 1.0 / jnp.sum(jnp.exp(gl - gmax), axis=0, keepdims=True)

    el = lt[EXPERT_ROW0:EXPERT_ROW0 + N_EXPERTS, :]
    row16 = lax.broadcasted_iota(jnp.int32, (N_EXPERTS, tm), 0)
    emask = (row16 >> (EXPERTS_PER_GROUP.bit_length() - 1)) == g_idx
    v1 = jnp.max(jnp.where(emask, el, neg), axis=0, keepdims=True)
    i1 = jnp.min(jnp.where(emask & (el == v1), row16, N_EXPERTS), axis=0, keepdims=True)
    emask2 = emask & (row16 != i1)
    v2 = jnp.max(jnp.where(emask2, el, neg), axis=0, keepdims=True)
    i2 = jnp.min(jnp.where(emask2 & (el == v2), row16, N_EXPERTS), axis=0, keepdims=True)
    e2 = jnp.exp(v2 - v1)
    p1 = 1.0 / (1.0 + e2)
    p2 = e2 * p1
    ids_ref[...] = jnp.where(row8 == 0, i1, jnp.where(row8 == 1, i2, 0))
    wts_ref[...] = jnp.where(row8 == 0, p1 * g_w, jnp.where(row8 == 1, p2 * g_w, 0.0))


def _router(x, g, mod, w_hi, w_lo, bias):
    tm = ROUTER_TM
    nt = TOKENS // tm
    return pl.pallas_call(
        _router_kernel,
        out_shape=(jax.ShapeDtypeStruct((TOKENS * PACK_ROWS, LANES), jnp.uint32),
                   jax.ShapeDtypeStruct((nt, SUBLANES, tm), jnp.int32),
                   jax.ShapeDtypeStruct((nt, SUBLANES, tm), F32)),
        grid=(nt,),
        in_specs=[
            pl.BlockSpec((tm, D_MODEL), lambda i: (i, 0)),
            pl.BlockSpec((1, D_MODEL), lambda i: (0, 0)),
            pl.BlockSpec((None, 6, D_MODEL), lambda i: (i // (SEQ // tm), 0, 0)),
            pl.BlockSpec((D_MODEL, ROUTER_LANES), lambda i: (0, 0)),
            pl.BlockSpec((D_MODEL, ROUTER_LANES), lambda i: (0, 0)),
            pl.BlockSpec((1, ROUTER_LANES), lambda i: (0, 0)),
        ],
        out_specs=(pl.BlockSpec((tm * PACK_ROWS, LANES), lambda i: (i, 0)),
                   pl.BlockSpec((None, SUBLANES, tm), lambda i: (i, 0, 0)),
                   pl.BlockSpec((None, SUBLANES, tm), lambda i: (i, 0, 0))),
        compiler_params=_params(("arbitrary",)),
        name="moe_router",
    )(x, g.reshape(1, D_MODEL), mod, w_hi, w_lo, bias)


def _plan_kernel(ids_ref, wts_ref, addr_ref, gw_ref, meta_ref, rank_scr):
    nc, _, tm = ids_ref.shape
    row = lax.broadcasted_iota(jnp.int32, (N_EXPERTS, tm), 0)
    upper = (lax.broadcasted_iota(jnp.int32, (tm, tm), 0)
             < lax.broadcasted_iota(jnp.int32, (tm, tm), 1)).astype(BF16)

    def member(c):
        e0 = ids_ref[c, 0:1, :]
        e1 = ids_ref[c, 1:2, :]
        return e0, e1, ((row == e0) | (row == e1))

    counts = jnp.zeros((N_EXPERTS, 1), F32)
    for c in range(nc):
        _, _, mem = member(c)
        memf = mem.astype(F32)
        rank_scr[c] = jnp.dot(memf.astype(BF16), upper, preferred_element_type=F32) + counts
        counts = counts + jnp.sum(memf, axis=1, keepdims=True)

    tiles = jnp.floor((counts + (ROW_GRAN - 1)) * (1.0 / ROW_GRAN))
    lower = (lax.broadcasted_iota(jnp.int32, (N_EXPERTS, N_EXPERTS), 1)
             < lax.broadcasted_iota(jnp.int32, (N_EXPERTS, N_EXPERTS), 0)).astype(BF16)
    tiles_b = jnp.broadcast_to(tiles, (N_EXPERTS, LANES))
    row_start = ROW_GRAN * jnp.dot(lower, tiles_b.astype(BF16), preferred_element_type=F32)
    meta_ref[0] = row_start.astype(jnp.int32)
    meta_ref[1] = (ROW_GRAN * tiles_b).astype(jnp.int32)

    rs = row_start[:, 0:1]
    for c in range(nc):
        e0, e1, _ = member(c)
        where = rs + rank_scr[c]
        for k, ek in enumerate((e0, e1)):
            pos = jnp.sum(jnp.where(row == ek, where, 0.0), axis=0, keepdims=True).astype(jnp.int32)
            addr = (pos >> GRAN_SHIFT) * TILE_ROWS + (pos & (ROW_GRAN - 1))
            cols = slice(k * MOE_BLOCK + c * tm, k * MOE_BLOCK + (c + 1) * tm)
            addr_ref[:, cols] = addr
            gw_ref[:, cols] = wts_ref[c, k:k + 1, :]


def _plan(ids, wts):
    tm = ROUTER_TM
    nc = MOE_BLOCK // tm
    nb = TOKENS // MOE_BLOCK
    return pl.pallas_call(
        _plan_kernel,
        out_shape=(jax.ShapeDtypeStruct((nb, 1, TOP_K * MOE_BLOCK), jnp.int32),
                   jax.ShapeDtypeStruct((nb, 1, TOP_K * MOE_BLOCK), F32),
                   jax.ShapeDtypeStruct((nb, 2, N_EXPERTS, LANES), jnp.int32)),
        grid=(nb,),
        in_specs=[pl.BlockSpec((nc, SUBLANES, tm), lambda b: (b, 0, 0)),
                  pl.BlockSpec((nc, SUBLANES, tm), lambda b: (b, 0, 0))],
        out_specs=(pl.BlockSpec((None, 1, TOP_K * MOE_BLOCK), lambda b: (b, 0, 0)),
                   pl.BlockSpec((None, 1, TOP_K * MOE_BLOCK), lambda b: (b, 0, 0)),
                   pl.BlockSpec((None, 2, N_EXPERTS, LANES), lambda b: (b, 0, 0, 0))),
        scratch_shapes=[pltpu.VMEM((nc, N_EXPERTS, tm), F32)],
        compiler_params=_params(("arbitrary",)),
        name="moe_plan",
    )(ids, wts)


def _expert_rows(xy_ref, w13_ref, w2_ref, off, n_tiles):
    tile0 = lax.shift_right_logical(off, GRAN_SHIFT)
    bases = [pl.multiple_of((tile0 + ti) * TILE_ROWS, SUBLANES) for ti in range(n_tiles)]
    lo_cols, hi_cols = [], []
    for j in range(PACK_ROWS):
        words = jnp.concatenate(
            [xy_ref[pl.ds(b + j * TILE_STRIDE, ROW_GRAN), :] for b in bases], axis=0)
        lo, hi = _unpack_words(words)
        lo_cols.append(lo.astype(BF16))
        hi_cols.append(hi.astype(BF16))
    xin = jnp.concatenate(lo_cols + hi_cols, axis=1)
    u = jnp.dot(xin, w13_ref[...], preferred_element_type=F32)
    u1 = u[:, :D_EXPERT]
    act = (u1 * jax.nn.sigmoid(u1) * u[:, D_EXPERT:]).astype(BF16)
    y = jnp.dot(act, w2_ref[...], preferred_element_type=F32)
    half = D_MODEL // 2
    for j in range(PACK_ROWS):
        words = _pack_words(y[:, j * LANES:(j + 1) * LANES],
                            y[:, half + j * LANES:half + (j + 1) * LANES])
        for ti, b in enumerate(bases):
            xy_ref[pl.ds(b + j * TILE_STRIDE, ROW_GRAN), :] = words[ti * ROW_GRAN:(ti + 1) * ROW_GRAN]


def _moe_kernel(rs_ref, pad_ref, hfp_ref, addr_ref, gw_ref, w13_ref, w2_ref, x_ref, mod_ref,
                fg_ref, o_ref, xy_scr, out8_scr, *, final_norm):
    b = pl.program_id(0)
    s = pl.program_id(1)
    unroll = SUBLANES

    @pl.when((b == 0) & (s == 0))
    def _():
        xy_scr[...] = jnp.zeros(xy_scr.shape, jnp.uint32)

    @pl.when(s == 0)
    def _():
        def group(g, carry):
            for i in range(unroll):
                t = g * unroll + i
                slab = hfp_ref[pl.ds(pl.multiple_of(g * (unroll * PACK_ROWS), unroll * PACK_ROWS)
                                     + i * PACK_ROWS, PACK_ROWS), :]
                for k in range(TOP_K):
                    xy_scr[pl.ds(addr_ref[0, k * MOE_BLOCK + t], PACK_ROWS, stride=TILE_STRIDE), :] = slab
            return carry
        lax.fori_loop(0, MOE_BLOCK // unroll, group, 0)

    def run_expert(slot):
        e = b * N_EXPERTS + s * EXPERTS_PER_STEP + slot
        row0 = rs_ref[e]
        padded = pad_ref[e]

        def super_tile(i, carry):
            off = row0 + i * SUPER_ROWS
            n_tiles = lax.shift_right_logical(
                jnp.minimum(padded - i * SUPER_ROWS, SUPER_ROWS), GRAN_SHIFT)
            for m in range(1, SUPER_ROWS // ROW_GRAN + 1):
                @pl.when(n_tiles == m)
                def _(m=m):
                    _expert_rows(xy_scr, w13_ref.at[slot], w2_ref.at[slot], off, m)
            return carry
        lax.fori_loop(0, lax.shift_right_logical(padded + (SUPER_ROWS - 1), SUPER_SHIFT), super_tile, 0)

    @pl.when(s < N_EXPERT_STEPS)
    def _():
        for slot in range(EXPERTS_PER_STEP):
            run_expert(slot)

    @pl.when(s >= N_EXPERT_STEPS)
    def _():
        t0 = (s - N_EXPERT_STEPS) * EPI_ROWS

        def group(g, carry):
            for i in range(unroll):
                tl = g * unroll + i
                lo_acc = hi_acc = None
                for k in range(TOP_K):
                    pair = k * MOE_BLOCK + t0 + tl
                    words = xy_scr[pl.ds(addr_ref[0, pair], PACK_ROWS, stride=TILE_STRIDE), :]
                    lo, hi = _unpack_words(words)
                    wk = gw_ref[0, pair]
                    lo_acc = wk * lo if lo_acc is None else lo_acc + wk * lo
                    hi_acc = wk * hi if hi_acc is None else hi_acc + wk * hi
                base = pl.multiple_of(g * (unroll * SUBLANES), unroll * SUBLANES) + i * SUBLANES
                out8_scr[pl.ds(base, PACK_ROWS), :] = lo_acc
                out8_scr[pl.ds(base + PACK_ROWS, PACK_ROWS), :] = hi_acc
            return carry
        lax.fori_loop(0, EPI_ROWS // unroll, group, 0)

        n_col = D_MODEL // LANES
        for g in range(EPI_ROWS // SUBLANES):
            for j in range(n_col):
                blk = out8_scr[pl.ds(g * SUBLANES * n_col + j, SUBLANES, stride=n_col), :]
                rows = slice(g * SUBLANES, (g + 1) * SUBLANES)
                cols = slice(j * LANES, (j + 1) * LANES)
                o_ref[rows, cols] = x_ref[rows, cols] + mod_ref[5:6, cols] * blk
        if final_norm:
            v = o_ref[...]
            o_ref[...] = v * lax.rsqrt(jnp.mean(v * v, axis=-1, keepdims=True) + NORM_EPS) * fg_ref[...]


def _moe_sparse(hfp, addr, gw, row_start, padded, w13, w2, x, mod, final_g, *, final_norm):
    nb = TOKENS // MOE_BLOCK
    steps = N_EXPERT_STEPS + N_EPI

    def epi_block(b, s, *_):
        return (b * N_EPI + jnp.maximum(s - N_EXPERT_STEPS, 0), 0)

    def expert_block(b, s, *_):
        return (jnp.minimum(s, N_EXPERT_STEPS - 1), 0, 0)

    grid_spec = pltpu.PrefetchScalarGridSpec(
        num_scalar_prefetch=2,
        grid=(nb, steps),
        in_specs=[
            pl.BlockSpec((MOE_BLOCK * PACK_ROWS, LANES), lambda b, s, *_: (b, 0)),
            pl.BlockSpec((None, 1, TOP_K * MOE_BLOCK), lambda b, s, *_: (b, 0, 0),
                         memory_space=pltpu.SMEM),
            pl.BlockSpec((None, 1, TOP_K * MOE_BLOCK), lambda b, s, *_: (b, 0, 0),
                         memory_space=pltpu.SMEM),
            pl.BlockSpec((EXPERTS_PER_STEP, D_MODEL, 2 * D_EXPERT), expert_block),
            pl.BlockSpec((EXPERTS_PER_STEP, D_EXPERT, D_MODEL), expert_block),
            pl.BlockSpec((EPI_ROWS, D_MODEL), epi_block),
            pl.BlockSpec((None, 6, D_MODEL), lambda b, s, *_: (b, 0, 0)),
            pl.BlockSpec((1, D_MODEL), lambda b, s, *_: (0, 0)),
        ],
        out_specs=pl.BlockSpec((EPI_ROWS, D_MODEL), epi_block),
        scratch_shapes=[
            pltpu.VMEM((N_TILES * TILE_ROWS, LANES), jnp.uint32),
            pltpu.VMEM((EPI_ROWS * SUBLANES, LANES), F32),
        ],
    )
    return pl.pallas_call(
        functools.partial(_moe_kernel, final_norm=final_norm),
        out_shape=jax.ShapeDtypeStruct((TOKENS, D_MODEL), F32),
        grid_spec=grid_spec,
        compiler_params=_params(("arbitrary", "arbitrary")),
        name="moe_experts",
    )(row_start, padded, hfp, addr, gw, w13, w2, x, mod,
      final_g.reshape(1, D_MODEL))


def _lambda_init(layer):
    return 0.8 - 0.6 * math.exp(-0.3 * layer)


def _split_bf16(w):
    hi = w.astype(BF16)
    return hi, (w - hi.astype(F32)).astype(BF16)


def kernel(x, c, norm_mix, norm_ffn, final_norm, ada_w, ada_b, attn_w_qkv, attn_lambda, attn_subln, attn_w_o, rec_w_in, rec_conv_w, rec_conv_b, rec_gate_w, rec_gate_b, rec_a_param, rec_w_out, moe_w_group, moe_b_group, moe_w_expert, moe_b_expert, moe_w13, moe_w2):
    xt = x.reshape(TOKENS, D_MODEL)
    mod_all = _ada_mod(c, ada_w, ada_b).reshape(DEPTH, BATCH, 6, D_MODEL)

    def lane_pad(n, *lead):
        return jnp.zeros(lead + (n,), F32)

    gap = EXPERT_ROW0 - N_GROUPS
    tail = ROUTER_LANES - EXPERT_ROW0 - N_EXPERTS
    w_router = jnp.concatenate([moe_w_group, lane_pad(gap, DEPTH, D_MODEL), moe_w_expert,
                                lane_pad(tail, DEPTH, D_MODEL)], axis=-1)
    b_router = jnp.concatenate([moe_b_group, lane_pad(gap, DEPTH), moe_b_expert,
                                lane_pad(tail, DEPTH)], axis=-1)
    wr_hi, wr_lo = _split_bf16(w_router)

    w_qkv = attn_w_qkv.astype(BF16)
    w_o = attn_w_o.astype(BF16)
    w_in = rec_w_in.astype(BF16)
    w_gate = rec_gate_w.astype(BF16)
    w_out = rec_w_out.astype(BF16)
    w13 = moe_w13.astype(BF16)
    w2 = moe_w2.astype(BF16)

    for layer in range(DEPTH):
        mod = mod_all[layer]
        j = layer // N_MIXERS
        if layer % N_MIXERS == 0:
            qkv = _norm_proj(xt, norm_mix[layer], mod, w_qkv[j], shift_row=0, scale_row=1,
                             out_scales=(Q_PRESCALE, 1.0, 1.0))
            o = _diff_attention(qkv, attn_lambda[j], attn_subln[j], _lambda_init(layer))
            xt = _proj_residual(o, w_o[j], xt, mod, gate_row=2)
        else:
            xt = _recurrent_layer(xt, norm_mix[layer], mod, w_in[j], rec_conv_w[j], rec_conv_b[j],
                                  w_gate[j], rec_gate_b[j], rec_a_param[j], w_out[j])
        hfp, ids, wts = _router(xt, norm_ffn[layer], mod, wr_hi[layer], wr_lo[layer],
                                b_router[layer].reshape(1, ROUTER_LANES))
        addr, gw, meta = _plan(ids, wts)
        xt = _moe_sparse(hfp, addr, gw, meta[:, 0, :, 0].reshape(-1), meta[:, 1, :, 0].reshape(-1),
                         w13[layer], w2[layer], xt, mod, final_norm,
                         final_norm=(layer == DEPTH - 1))

    return xt.reshape(BATCH, SEQ, D_MODEL)
```

```python
import functools
import math

import jax
import jax.numpy as jnp
from jax import lax
from jax.experimental import pallas as pl
from jax.experimental.pallas import tpu as pltpu

D_MODEL = 1024
BATCH = 32
SEQ = 2048
TOKENS = BATCH * SEQ
DEPTH = 4
N_MIXERS = 2
HEAD_DIM = 64
ATTN_HEADS = D_MODEL // (2 * HEAD_DIM)
ATTN_WIDTH = ATTN_HEADS * 2 * HEAD_DIM
SUBLN_EPS = 1e-5
D_RNN = D_MODEL
RNN_HEADS = 4
RNN_BLOCK = D_RNN // RNN_HEADS
CONV_WIDTH = 4
RG_LRU_C = 8.0
N_GROUPS = 4
EXPERTS_PER_GROUP = 4
N_EXPERTS = N_GROUPS * EXPERTS_PER_GROUP
TOP_K = 2
D_EXPERT = 512
NORM_EPS = 1e-6
Q_PRESCALE = HEAD_DIM ** -0.5 * math.log2(math.e)

LANES = 128
SUBLANES = 8
ROUTER_LANES = LANES
SCAN_PITCH_PAD = 4
VMEM_LIMIT = 56 * 1024 * 1024

F32 = jnp.float32
BF16 = jnp.bfloat16


def _params(semantics):
    return pltpu.CompilerParams(dimension_semantics=semantics, vmem_limit_bytes=VMEM_LIMIT)


def _rms_mod(x, g, shift, scale, eps=NORM_EPS):
    y = x * lax.rsqrt(jnp.mean(x * x, axis=-1, keepdims=True) + eps) * g
    return y * (1.0 + scale) + shift


def _ada_kernel(c_ref, w_ref, b_ref, o_ref):
    c = c_ref[...]
    cond = (c * jax.nn.sigmoid(c)).astype(BF16)
    o_ref[...] = jnp.dot(cond, w_ref[...].astype(BF16), preferred_element_type=F32) + b_ref[...]


def _ada_mod(c, ada_w, ada_b):
    tn = 1536
    n = 6 * D_MODEL
    return pl.pallas_call(
        _ada_kernel,
        out_shape=jax.ShapeDtypeStruct((DEPTH, BATCH, n), F32),
        grid=(DEPTH, n // tn),
        in_specs=[
            pl.BlockSpec((BATCH, D_MODEL), lambda l, j: (0, 0)),
            pl.BlockSpec((None, D_MODEL, tn), lambda l, j: (l, 0, j)),
            pl.BlockSpec((None, 1, tn), lambda l, j: (l, 0, j)),
        ],
        out_specs=pl.BlockSpec((None, BATCH, tn), lambda l, j: (l, 0, j)),
        compiler_params=_params(("arbitrary", "arbitrary")),
        name="ada_mod",
    )(c, ada_w, ada_b.reshape(DEPTH, 1, n))


def _norm_proj_kernel(x_ref, g_ref, mod_ref, w_ref, o_ref, *, shift_row, scale_row, tn, out_scales):
    h = _rms_mod(x_ref[...], g_ref[...], mod_ref[shift_row:shift_row + 1, :],
                 mod_ref[scale_row:scale_row + 1, :]).astype(BF16)
    for j in range(o_ref.shape[1] // tn):
        y = jnp.dot(h, w_ref[:, j * tn:(j + 1) * tn], preferred_element_type=F32)
        if out_scales[j] != 1.0:
            y = y * out_scales[j]
        o_ref[:, j * tn:(j + 1) * tn] = y.astype(o_ref.dtype)


def _norm_proj(x, g, mod, w, *, shift_row, scale_row, out_scales, tm=512, tn=1024):
    n = w.shape[1]
    assert len(out_scales) == n // tn
    return pl.pallas_call(
        functools.partial(_norm_proj_kernel, shift_row=shift_row, scale_row=scale_row, tn=tn,
                          out_scales=out_scales),
        out_shape=jax.ShapeDtypeStruct((TOKENS, n), BF16),
        grid=(TOKENS // tm,),
        in_specs=[
            pl.BlockSpec((tm, D_MODEL), lambda i: (i, 0)),
            pl.BlockSpec((1, D_MODEL), lambda i: (0, 0)),
            pl.BlockSpec((None, 6, D_MODEL), lambda i: (i // (SEQ // tm), 0, 0)),
            pl.BlockSpec((D_MODEL, n), lambda i: (0, 0)),
        ],
        out_specs=pl.BlockSpec((tm, n), lambda i: (i, 0)),
        compiler_params=_params(("arbitrary",)),
        name="norm_proj",
    )(x, g.reshape(1, D_MODEL), mod, w)


def _attn_kernel(lam_ref, sg_ref, q_ref, k_ref, v_ref, o_ref, qm_scr, ve_scr, m_scr, acc_scr,
                 *, tq, lambda_init):
    qi = pl.program_id(1)
    hw = 2 * HEAD_DIM
    lv = lam_ref[...]
    lam = (jnp.exp(jnp.sum(lv[0:1] * lv[1:2], keepdims=True))
           - jnp.exp(jnp.sum(lv[2:3] * lv[3:4], keepdims=True)) + lambda_init)

    lane = lax.broadcasted_iota(jnp.int32, (tq, hw), 1)
    for h in range(ATTN_HEADS):
        q = q_ref[:, h * hw:(h + 1) * hw]
        qm_scr[2 * h] = jnp.where(lane < HEAD_DIM, q, jnp.zeros_like(q))
        qm_scr[2 * h + 1] = jnp.where(lane >= HEAD_DIM, q, jnp.zeros_like(q))

    @pl.when(qi == 0)
    def _():
        ones = jnp.ones((SEQ, hw), BF16)
        for h in range(ATTN_HEADS):
            ve_scr[:, 2 * h * hw:(2 * h + 1) * hw] = v_ref[:, h * hw:(h + 1) * hw]
            ve_scr[:, (2 * h + 1) * hw:(2 * h + 2) * hw] = ones

    def step(j, masked, first):
        start = pl.multiple_of(j * tq, tq)
        if masked:
            row = lax.broadcasted_iota(jnp.int32, (tq, tq), 0)
            col = lax.broadcasted_iota(jnp.int32, (tq, tq), 1)
            keep = row >= col
        for h in range(ATTN_HEADS):
            kb = k_ref[pl.ds(start, tq), h * hw:(h + 1) * hw]
            vb = ve_scr[pl.ds(start, tq), 2 * h * hw:(2 * h + 2) * hw]
            for mi in range(2):
                c = 2 * h + mi
                s = lax.dot_general(qm_scr[c], kb, (((1,), (1,)), ((), ())),
                                    preferred_element_type=F32)
                if masked:
                    s = jnp.where(keep, s, -jnp.inf)
                m_cur = jnp.max(s, axis=-1, keepdims=True)
                if first:
                    m_new = jnp.broadcast_to(m_cur, (tq, LANES))
                else:
                    m_prev = m_scr[c]
                    m_new = jnp.maximum(m_prev, m_cur)
                    alpha = jnp.exp2(m_prev - m_new)
                p = jnp.exp2(s - jnp.concatenate([m_new] * (tq // LANES), axis=1))
                pv = jnp.dot(p.astype(BF16), vb, preferred_element_type=F32)
                if first:
                    acc_scr[c] = pv
                else:
                    acc_scr[c] = jnp.concatenate([alpha, alpha], axis=1) * acc_scr[c] + pv
                m_scr[c] = m_new

    step(qi, True, True)

    def body(j, carry):
        step(j, False, False)
        return carry

    lax.fori_loop(0, qi, body, 0)

    for h in range(ATTN_HEADS):
        a1 = acc_scr[2 * h]
        a2 = acc_scr[2 * h + 1]
        o = a1[:, :hw] / a1[:, hw:] - lam * (a2[:, :hw] / a2[:, hw:])
        o = o * lax.rsqrt(jnp.mean(o * o, axis=-1, keepdims=True) + SUBLN_EPS) * sg_ref[...]
        o_ref[:, h * hw:(h + 1) * hw] = (o * (1.0 - lambda_init)).astype(o_ref.dtype)


def _diff_attention(qkv, lam_vec, subln_g, lambda_init, *, tq=256):
    nq = SEQ // tq
    hw = 2 * HEAD_DIM
    assert hw == LANES
    chains = 2 * ATTN_HEADS
    return pl.pallas_call(
        functools.partial(_attn_kernel, tq=tq, lambda_init=lambda_init),
        out_shape=jax.ShapeDtypeStruct((TOKENS, ATTN_WIDTH), BF16),
        grid=(BATCH, nq),
        in_specs=[
            pl.BlockSpec((4, HEAD_DIM), lambda b, i: (0, 0)),
            pl.BlockSpec((1, hw), lambda b, i: (0, 0)),
            pl.BlockSpec((tq, ATTN_WIDTH), lambda b, i: (b * nq + i, 0)),
            pl.BlockSpec((SEQ, ATTN_WIDTH), lambda b, i: (b, 1)),
            pl.BlockSpec((SEQ, ATTN_WIDTH), lambda b, i: (b, 2)),
        ],
        out_specs=pl.BlockSpec((tq, ATTN_WIDTH), lambda b, i: (b * nq + i, 0)),
        scratch_shapes=[
            pltpu.VMEM((chains, tq, hw), BF16),
            pltpu.VMEM((SEQ, 2 * ATTN_WIDTH), BF16),
            pltpu.VMEM((chains, tq, LANES), F32),
            pltpu.VMEM((chains, tq, 2 * hw), F32),
        ],
        compiler_params=_params(("arbitrary", "arbitrary")),
        name="diff_attention",
    )(lam_vec, subln_g.reshape(1, hw), qkv, qkv, qkv)


def _proj_residual_kernel(a_ref, w_ref, x_ref, mod_ref, o_ref, *, gate_row):
    y = jnp.dot(a_ref[...], w_ref[...], preferred_element_type=F32)
    o_ref[...] = x_ref[...] + mod_ref[gate_row:gate_row + 1, :] * y


def _proj_residual(a, w, x, mod, *, gate_row, tm=512):
    k = a.shape[1]
    return pl.pallas_call(
        functools.partial(_proj_residual_kernel, gate_row=gate_row),
        out_shape=jax.ShapeDtypeStruct((TOKENS, D_MODEL), F32),
        grid=(TOKENS // tm,),
        in_specs=[
            pl.BlockSpec((tm, k), lambda i: (i, 0)),
            pl.BlockSpec((k, D_MODEL), lambda i: (0, 0)),
            pl.BlockSpec((tm, D_MODEL), lambda i: (i, 0)),
            pl.BlockSpec((None, 6, D_MODEL), lambda i: (i // (SEQ // tm), 0, 0)),
        ],
        out_specs=pl.BlockSpec((tm, D_MODEL), lambda i: (i, 0)),
        compiler_params=_params(("arbitrary",)),
        name="proj_residual",
    )(a, w, x, mod)


def _gelu_tanh(y):
    c = math.sqrt(2.0 / math.pi)
    return 0.5 * y * (1.0 + jnp.tanh(c * (y + 0.044715 * (y * y * y))))


def _rec_kernel(x_ref, g_ref, mod_ref, w_in_ref, conv_w_ref, conv_b_ref, gate_w_ref, gate_b_ref,
                a_param_ref, w_out_ref, o_ref, ext_scr, a_scr, b_scr, h_scr, ac_scr, carry_scr, *, tm):
    t = pl.program_id(1)
    seg = tm // SUBLANES
    pitch = seg + SCAN_PITCH_PAD
    n_lt = D_RNN // LANES

    @pl.when(t == 0)
    def _():
        carry_scr[...] = jnp.zeros(carry_scr.shape, F32)
        ext_scr[0:SUBLANES, :] = jnp.zeros((SUBLANES, D_RNN), F32)

    x = x_ref[...]
    hm = _rms_mod(x, g_ref[...], mod_ref[0:1, :], mod_ref[1:2, :]).astype(BF16)
    y = jnp.dot(hm, w_in_ref[:, :D_RNN], preferred_element_type=F32)
    xr = jnp.dot(hm, w_in_ref[:, D_RNN:], preferred_element_type=F32)

    ext_scr[SUBLANES:SUBLANES + tm, :] = xr
    xc = conv_b_ref[...] + conv_w_ref[CONV_WIDTH - 1:CONV_WIDTH, :] * xr
    for k in range(CONV_WIDTH - 1):
        back = CONV_WIDTH - 1 - k
        xc = xc + conv_w_ref[k:k + 1, :] * ext_scr[SUBLANES - back:SUBLANES - back + tm, :]
    ext_scr[0:SUBLANES, :] = xr[tm - SUBLANES:, :]

    z = -a_param_ref[...]
    softplus = jnp.maximum(z, 0.0) + jnp.log(1.0 + jnp.exp(-jnp.abs(z)))
    xcb = xc.astype(BF16)
    for hd in range(RNN_HEADS):
        sl = slice(hd * RNN_BLOCK, (hd + 1) * RNN_BLOCK)
        gates = jnp.dot(xcb[:, sl], gate_w_ref[hd], preferred_element_type=F32) + gate_b_ref[hd]
        gates = jax.nn.sigmoid(gates)
        r = gates[:, :RNN_BLOCK]
        i = gates[:, RNN_BLOCK:]
        log_a = -RG_LRU_C * r * softplus[:, sl]
        a = jnp.exp(log_a)
        b = jnp.sqrt(1.0 - a * a) * (i * xc[:, sl])
        for jj in range(RNN_BLOCK // LANES):
            lt = hd * (RNN_BLOCK // LANES) + jj
            for sg in range(SUBLANES):
                rows = slice(sg * pitch, sg * pitch + seg)
                a_scr[lt, rows, :] = a[sg * seg:(sg + 1) * seg, jj * LANES:(jj + 1) * LANES]
                b_scr[lt, rows, :] = b[sg * seg:(sg + 1) * seg, jj * LANES:(jj + 1) * LANES]

    def scan_body(v, carry):
        h, ac = carry
        step = pl.ds(v, SUBLANES, stride=pitch)
        a = jnp.concatenate([a_scr[lt, step, :] for lt in range(n_lt)], axis=1)
        b = jnp.concatenate([b_scr[lt, step, :] for lt in range(n_lt)], axis=1)
        h = a * h + b
        ac = a * ac
        for lt in range(n_lt):
            h_scr[lt, step, :] = h[:, lt * LANES:(lt + 1) * LANES]
            ac_scr[lt, step, :] = ac[:, lt * LANES:(lt + 1) * LANES]
        return h, ac

    h_end, a_end = lax.fori_loop(
        0, seg, scan_body,
        (jnp.zeros((SUBLANES, D_RNN), F32), jnp.ones((SUBLANES, D_RNN), F32)))

    sub = lax.broadcasted_iota(jnp.int32, (SUBLANES, D_RNN), 0)
    prev = carry_scr[...]
    a, b = a_end, h_end
    for d in (1, 2, 4):
        keep = sub >= d
        a_sh = jnp.where(keep, pltpu.roll(a, d, axis=0), 1.0)
        b_sh = jnp.where(keep, pltpu.roll(b, d, axis=0), 0.0)
        b = a * b_sh + b
        a = a * a_sh
    seg_end = a * prev + b
    seg_in = jnp.where(sub >= 1, pltpu.roll(seg_end, 1, axis=0), prev)
    carry_scr[...] = jnp.broadcast_to(seg_end[SUBLANES - 1:SUBLANES, :], (SUBLANES, D_RNN))

    hs = jnp.concatenate(
        [jnp.concatenate(
            [h_scr[lt, sg * pitch:sg * pitch + seg, :]
             + ac_scr[lt, sg * pitch:sg * pitch + seg, :] * seg_in[sg:sg + 1, lt * LANES:(lt + 1) * LANES]
             for lt in range(n_lt)], axis=1)
         for sg in range(SUBLANES)], axis=0)
    mixed = (_gelu_tanh(y) * hs).astype(BF16)
    out = jnp.dot(mixed, w_out_ref[...], preferred_element_type=F32)
    o_ref[...] = x + mod_ref[2:3, :] * out


def _recurrent_layer(x, g, mod, w_in, conv_w, conv_b, gate_w, gate_b, a_param, w_out, *, tm=512):
    nt = SEQ // tm
    const2 = lambda b, t: (0, 0)
    const3 = lambda b, t: (0, 0, 0)
    return pl.pallas_call(
        functools.partial(_rec_kernel, tm=tm),
        out_shape=jax.ShapeDtypeStruct((TOKENS, D_MODEL), F32),
        grid=(BATCH, nt),
        in_specs=[
            pl.BlockSpec((tm, D_MODEL), lambda b, t: (b * nt + t, 0)),
            pl.BlockSpec((1, D_MODEL), const2),
            pl.BlockSpec((None, 6, D_MODEL), lambda b, t: (b, 0, 0)),
            pl.BlockSpec((D_MODEL, 2 * D_RNN), const2),
            pl.BlockSpec((CONV_WIDTH, D_RNN), const2),
            pl.BlockSpec((1, D_RNN), const2),
            pl.BlockSpec((RNN_HEADS, RNN_BLOCK, 2 * RNN_BLOCK), const3),
            pl.BlockSpec((RNN_HEADS, 1, 2 * RNN_BLOCK), const3),
            pl.BlockSpec((1, D_RNN), const2),
            pl.BlockSpec((D_RNN, D_MODEL), const2),
        ],
        out_specs=pl.BlockSpec((tm, D_MODEL), lambda b, t: (b * nt + t, 0)),
        scratch_shapes=[
            pltpu.VMEM((tm + SUBLANES, D_RNN), F32),
            pltpu.VMEM((D_RNN // LANES, tm + SUBLANES * SUBLANES, LANES), F32),
            pltpu.VMEM((D_RNN // LANES, tm + SUBLANES * SUBLANES, LANES), F32),
            pltpu.VMEM((D_RNN // LANES, tm + SUBLANES * SUBLANES, LANES), F32),
            pltpu.VMEM((D_RNN // LANES, tm + SUBLANES * SUBLANES, LANES), F32),
            pltpu.VMEM((SUBLANES, D_RNN), F32),
        ],
        compiler_params=_params(("arbitrary", "arbitrary")),
        name="recurrent_layer",
    )(x, g.reshape(1, D_MODEL), mod, w_in, conv_w, conv_b.reshape(1, D_RNN), gate_w,
      gate_b.reshape(RNN_HEADS, 1, 2 * RNN_BLOCK), a_param.reshape(1, D_RNN), w_out)


MOE_BLOCK = SEQ
PACK_ROWS = D_MODEL // (2 * LANES)
ROW_GRAN = 64
GRAN_SHIFT = ROW_GRAN.bit_length() - 1
TILE_STRIDE = ROW_GRAN + SUBLANES
TILE_ROWS = PACK_ROWS * TILE_STRIDE
N_TILES = TOP_K * MOE_BLOCK // ROW_GRAN + N_EXPERTS
SUPER_ROWS = 512
SUPER_SHIFT = SUPER_ROWS.bit_length() - 1
EXPERTS_PER_STEP = 2
N_EXPERT_STEPS = N_EXPERTS // EXPERTS_PER_STEP
EPI_ROWS = 512
N_EPI = MOE_BLOCK // EPI_ROWS
ROUTER_TM = 512
GROUP_ROW0 = 0
EXPERT_ROW0 = SUBLANES
HI_MASK = 0xFFFF0000


def _pack_words(lo, hi):
    lo_bits = lax.bitcast_convert_type(lo.astype(BF16).astype(F32), jnp.uint32)
    hi_bits = lax.bitcast_convert_type(hi.astype(BF16).astype(F32), jnp.uint32)
    return (hi_bits & jnp.uint32(HI_MASK)) | (lo_bits >> jnp.uint32(16))


def _unpack_words(w):
    lo = lax.bitcast_convert_type(w << jnp.uint32(16), F32)
    hi = lax.bitcast_convert_type(w & jnp.uint32(HI_MASK), F32)
    return lo, hi


def _router_kernel(x_ref, g_ref, mod_ref, w_hi_ref, w_lo_ref, b_ref, hfp_ref, ids_ref, wts_ref):
    tm = x_ref.shape[0]
    hf = _rms_mod(x_ref[...], g_ref[...], mod_ref[3:4, :], mod_ref[4:5, :])
    hi = hf.astype(BF16)
    hi32 = hi.astype(F32)
    half = D_MODEL // 2
    for j in range(PACK_ROWS):
        lo_bits = lax.bitcast_convert_type(hi32[:, j * LANES:(j + 1) * LANES], jnp.uint32)
        hi_bits = lax.bitcast_convert_type(hi32[:, half + j * LANES:half + (j + 1) * LANES],
                                           jnp.uint32)
        words = (hi_bits & jnp.uint32(HI_MASK)) | (lo_bits >> jnp.uint32(16))
        hfp_ref[pl.ds(j, tm, stride=PACK_ROWS), :] = words

    lo = (hf - hi32).astype(BF16)
    logits = (jnp.dot(hi, w_hi_ref[...], preferred_element_type=F32)
              + jnp.dot(lo, w_hi_ref[...], preferred_element_type=F32)
              + jnp.dot(hi, w_lo_ref[...], preferred_element_type=F32)) + b_ref[...]
    lt = logits.T

    neg = -jnp.inf
    row8 = lax.broadcasted_iota(jnp.int32, (SUBLANES, tm), 0)
    gl = jnp.where(row8 < N_GROUPS, lt[GROUP_ROW0:GROUP_ROW0 + SUBLANES, :], neg)
    gmax = jnp.max(gl, axis=0, keepdims=True)
    g_idx = jnp.min(jnp.where(gl == gmax, row8, SUBLANES), axis=0, keepdims=True)
    g_w = 1.0 / jnp.sum(jnp.exp(gl - gmax), axis=0, keepdims=True)

    el = lt[EXPERT_ROW0:EXPERT_ROW0 + N_EXPERTS, :]
    row16 = lax.broadcasted_iota(jnp.int32, (N_EXPERTS, tm), 0)
    emask = (row16 >> (EXPERTS_PER_GROUP.bit_length() - 1)) == g_idx
    v1 = jnp.max(jnp.where(emask, el, neg), axis=0, keepdims=True)
    i1 = jnp.min(jnp.where(emask & (el == v1), row16, N_EXPERTS), axis=0, keepdims=True)
    emask2 = emask & (row16 != i1)
    v2 = jnp.max(jnp.where(emask2, el, neg), axis=0, keepdims=True)
    i2 = jnp.min(jnp.where(emask2 & (el == v2), row16, N_EXPERTS), axis=0, keepdims=True)
    e2 = jnp.exp(v2 - v1)
    p1 = 1.0 / (1.0 + e2)
    p2 = e2 * p1
    ids_ref[...] = jnp.where(row8 == 0, i1, jnp.where(row8 == 1, i2, 0))
    wts_ref[...] = jnp.where(row8 == 0, p1 * g_w, jnp.where(row8 == 1, p2 * g_w, 0.0))


def _router(x, g, mod, w_hi, w_lo, bias):
    tm = ROUTER_TM
    nt = TOKENS // tm
    return pl.pallas_call(
        _router_kernel,
        out_shape=(jax.ShapeDtypeStruct((TOKENS * PACK_ROWS, LANES), jnp.uint32),
                   jax.ShapeDtypeStruct((nt, SUBLANES, tm), jnp.int32),
                   jax.ShapeDtypeStruct((nt, SUBLANES, tm), F32)),
        grid=(nt,),
        in_specs=[
            pl.BlockSpec((tm, D_MODEL), lambda i: (i, 0)),
            pl.BlockSpec((1, D_MODEL), lambda i: (0, 0)),
            pl.BlockSpec((None, 6, D_MODEL), lambda i: (i // (SEQ // tm), 0, 0)),
            pl.BlockSpec((D_MODEL, ROUTER_LANES), lambda i: (0, 0)),
            pl.BlockSpec((D_MODEL, ROUTER_LANES), lambda i: (0, 0)),
            pl.BlockSpec((1, ROUTER_LANES), lambda i: (0, 0)),
        ],
        out_specs=(pl.BlockSpec((tm * PACK_ROWS, LANES), lambda i: (i, 0)),
                   pl.BlockSpec((None, SUBLANES, tm), lambda i: (i, 0, 0)),
                   pl.BlockSpec((None, SUBLANES, tm), lambda i: (i, 0, 0))),
        compiler_params=_params(("arbitrary",)),
        name="moe_router",
    )(x, g.reshape(1, D_MODEL), mod, w_hi, w_lo, bias)


def _plan_kernel(ids_ref, wts_ref, addr_ref, gw_ref, meta_ref, rank_scr):
    nc, _, tm = ids_ref.shape
    row = lax.broadcasted_iota(jnp.int32, (N_EXPERTS, tm), 0)
    upper = (lax.broadcasted_iota(jnp.int32, (tm, tm), 0)
             < lax.broadcasted_iota(jnp.int32, (tm, tm), 1)).astype(BF16)

    def member(c):
        e0 = ids_ref[c, 0:1, :]
        e1 = ids_ref[c, 1:2, :]
        return e0, e1, ((row == e0) | (row == e1))

    counts = jnp.zeros((N_EXPERTS, 1), F32)
    for c in range(nc):
        _, _, mem = member(c)
        memf = mem.astype(F32)
        rank_scr[c] = jnp.dot(memf.astype(BF16), upper, preferred_element_type=F32) + counts
        counts = counts + jnp.sum(memf, axis=1, keepdims=True)

    tiles = jnp.floor((counts + (ROW_GRAN - 1)) * (1.0 / ROW_GRAN))
    lower = (lax.broadcasted_iota(jnp.int32, (N_EXPERTS, N_EXPERTS), 1)
             < lax.broadcasted_iota(jnp.int32, (N_EXPERTS, N_EXPERTS), 0)).astype(BF16)
    tiles_b = jnp.broadcast_to(tiles, (N_EXPERTS, LANES))
    row_start = ROW_GRAN * jnp.dot(lower, tiles_b.astype(BF16), preferred_element_type=F32)
    meta_ref[0] = row_start.astype(jnp.int32)
    meta_ref[1] = (ROW_GRAN * tiles_b).astype(jnp.int32)

    rs = row_start[:, 0:1]
    for c in range(nc):
        e0, e1, _ = member(c)
        where = rs + rank_scr[c]
        for k, ek in enumerate((e0, e1)):
            pos = jnp.sum(jnp.where(row == ek, where, 0.0), axis=0, keepdims=True).astype(jnp.int32)
            addr = (pos >> GRAN_SHIFT) * TILE_ROWS + (pos & (ROW_GRAN - 1))
            cols = slice(k * MOE_BLOCK + c * tm, k * MOE_BLOCK + (c + 1) * tm)
            addr_ref[:, cols] = addr
            gw_ref[:, cols] = wts_ref[c, k:k + 1, :]


def _plan(ids, wts):
    tm = ROUTER_TM
    nc = MOE_BLOCK // tm
    nb = TOKENS // MOE_BLOCK
    return pl.pallas_call(
        _plan_kernel,
        out_shape=(jax.ShapeDtypeStruct((nb, 1, TOP_K * MOE_BLOCK), jnp.int32),
                   jax.ShapeDtypeStruct((nb, 1, TOP_K * MOE_BLOCK), F32),
                   jax.ShapeDtypeStruct((nb, 2, N_EXPERTS, LANES), jnp.int32)),
        grid=(nb,),
        in_specs=[pl.BlockSpec((nc, SUBLANES, tm), lambda b: (b, 0, 0)),
                  pl.BlockSpec((nc, SUBLANES, tm), lambda b: (b, 0, 0))],
        out_specs=(pl.BlockSpec((None, 1, TOP_K * MOE_BLOCK), lambda b: (b, 0, 0)),
                   pl.BlockSpec((None, 1, TOP_K * MOE_BLOCK), lambda b: (b, 0, 0)),
                   pl.BlockSpec((None, 2, N_EXPERTS, LANES), lambda b: (b, 0, 0, 0))),
        scratch_shapes=[pltpu.VMEM((nc, N_EXPERTS, tm), F32)],
        compiler_params=_params(("arbitrary",)),
        name="moe_plan",
    )(ids, wts)


def _expert_rows(xy_ref, w13_ref, w2_ref, off, n_tiles):
    tile0 = lax.shift_right_logical(off, GRAN_SHIFT)
    bases = [pl.multiple_of((tile0 + ti) * TILE_ROWS, SUBLANES) for ti in range(n_tiles)]
    lo_cols, hi_cols = [], []
    for j in range(PACK_ROWS):
        words = jnp.concatenate(
            [xy_ref[pl.ds(b + j * TILE_STRIDE, ROW_GRAN), :] for b in bases], axis=0)
        lo, hi = _unpack_words(words)
        lo_cols.append(lo.astype(BF16))
        hi_cols.append(hi.astype(BF16))
    xin = jnp.concatenate(lo_cols + hi_cols, axis=1)
    u = jnp.dot(xin, w13_ref[...], preferred_element_type=F32)
    u1 = u[:, :D_EXPERT]
    act = (u1 * jax.nn.sigmoid(u1) * u[:, D_EXPERT:]).astype(BF16)
    y = jnp.dot(act, w2_ref[...], preferred_element_type=F32)
    half = D_MODEL // 2
    for j in range(PACK_ROWS):
        words = _pack_words(y[:, j * LANES:(j + 1) * LANES],
                            y[:, half + j * LANES:half + (j + 1) * LANES])
        for ti, b in enumerate(bases):
            xy_ref[pl.ds(b + j * TILE_STRIDE, ROW_GRAN), :] = words[ti * ROW_GRAN:(ti + 1) * ROW_GRAN]


def _moe_kernel(rs_ref, pad_ref, hfp_ref, addr_ref, gw_ref, w13_ref, w2_ref, x_ref, mod_ref,
                fg_ref, o_ref, xy_scr, out8_scr, *, final_norm):
    b = pl.program_id(0)
    s = pl.program_id(1)
    unroll = SUBLANES

    @pl.when((b == 0) & (s == 0))
    def _():
        xy_scr[...] = jnp.zeros(xy_scr.shape, jnp.uint32)

    @pl.when(s == 0)
    def _():
        def group(g, carry):
            for i in range(unroll):
                t = g * unroll + i
                slab = hfp_ref[pl.ds(pl.multiple_of(g * (unroll * PACK_ROWS), unroll * PACK_ROWS)
                                     + i * PACK_ROWS, PACK_ROWS), :]
                for k in range(TOP_K):
                    xy_scr[pl.ds(addr_ref[0, k * MOE_BLOCK + t], PACK_ROWS, stride=TILE_STRIDE), :] = slab
            return carry
        lax.fori_loop(0, MOE_BLOCK // unroll, group, 0)

    def run_expert(slot):
        e = b * N_EXPERTS + s * EXPERTS_PER_STEP + slot
        row0 = rs_ref[e]
        padded = pad_ref[e]

        def super_tile(i, carry):
            off = row0 + i * SUPER_ROWS
            n_tiles = lax.shift_right_logical(
                jnp.minimum(padded - i * SUPER_ROWS, SUPER_ROWS), GRAN_SHIFT)
            for m in range(1, SUPER_ROWS // ROW_GRAN + 1):
                @pl.when(n_tiles == m)
                def _(m=m):
                    _expert_rows(xy_scr, w13_ref.at[slot], w2_ref.at[slot], off, m)
            return carry
        lax.fori_loop(0, lax.shift_right_logical(padded + (SUPER_ROWS - 1), SUPER_SHIFT), super_tile, 0)

    @pl.when(s < N_EXPERT_STEPS)
    def _():
        for slot in range(EXPERTS_PER_STEP):
            run_expert(slot)

    @pl.when(s >= N_EXPERT_STEPS)
    def _():
        t0 = (s - N_EXPERT_STEPS) * EPI_ROWS

        def group(g, carry):
            for i in range(unroll):
                tl = g * unroll + i
                lo_acc = hi_acc = None
                for k in range(TOP_K):
                    pair = k * MOE_BLOCK + t0 + tl
                    words = xy_scr[pl.ds(addr_ref[0, pair], PACK_ROWS, stride=TILE_STRIDE), :]
                    lo, hi = _unpack_words(words)
                    wk = gw_ref[0, pair]
                    lo_acc = wk * lo if lo_acc is None else lo_acc + wk * lo
                    hi_acc = wk * hi if hi_acc is None else hi_acc + wk * hi
                base = pl.multiple_of(g * (unroll * SUBLANES), unroll * SUBLANES) + i * SUBLANES
                out8_scr[pl.ds(base, PACK_ROWS), :] = lo_acc
                out8_scr[pl.ds(base + PACK_ROWS, PACK_ROWS), :] = hi_acc
            return carry
        lax.fori_loop(0, EPI_ROWS // unroll, group, 0)

        n_col = D_MODEL // LANES
        for g in range(EPI_ROWS // SUBLANES):
            for j in range(n_col):
                blk = out8_scr[pl.ds(g * SUBLANES * n_col + j, SUBLANES, stride=n_col), :]
                rows = slice(g * SUBLANES, (g + 1) * SUBLANES)
                cols = slice(j * LANES, (j + 1) * LANES)
                o_ref[rows, cols] = x_ref[rows, cols] + mod_ref[5:6, cols] * blk
        if final_norm:
            v = o_ref[...]
            o_ref[...] = v * lax.rsqrt(jnp.mean(v * v, axis=-1, keepdims=True) + NORM_EPS) * fg_ref[...]


def _moe_sparse(hfp, addr, gw, row_start, padded, w13, w2, x, mod, final_g, *, layer, final_norm):
    nb = TOKENS // MOE_BLOCK
    steps = N_EXPERT_STEPS + N_EPI

    def epi_block(b, s, *_):
        return (b * N_EPI + jnp.maximum(s - N_EXPERT_STEPS, 0), 0)

    def expert_block(b, s, *_):
        return (layer, jnp.minimum(s, N_EXPERT_STEPS - 1), 0, 0)

    grid_spec = pltpu.PrefetchScalarGridSpec(
        num_scalar_prefetch=2,
        grid=(nb, steps),
        in_specs=[
            pl.BlockSpec((MOE_BLOCK * PACK_ROWS, LANES), lambda b, s, *_: (b, 0)),
            pl.BlockSpec((None, 1, TOP_K * MOE_BLOCK), lambda b, s, *_: (b, 0, 0),
                         memory_space=pltpu.SMEM),
            pl.BlockSpec((None, 1, TOP_K * MOE_BLOCK), lambda b, s, *_: (b, 0, 0),
                         memory_space=pltpu.SMEM),
            pl.BlockSpec((None, EXPERTS_PER_STEP, D_MODEL, 2 * D_EXPERT), expert_block),
            pl.BlockSpec((None, EXPERTS_PER_STEP, D_EXPERT, D_MODEL), expert_block),
            pl.BlockSpec((EPI_ROWS, D_MODEL), epi_block),
            pl.BlockSpec((None, 6, D_MODEL), lambda b, s, *_: (b, 0, 0)),
            pl.BlockSpec((1, D_MODEL), lambda b, s, *_: (0, 0)),
        ],
        out_specs=pl.BlockSpec((EPI_ROWS, D_MODEL), epi_block),
        scratch_shapes=[
            pltpu.VMEM((N_TILES * TILE_ROWS, LANES), jnp.uint32),
            pltpu.VMEM((EPI_ROWS * SUBLANES, LANES), F32),
        ],
    )
    return pl.pallas_call(
        functools.partial(_moe_kernel, final_norm=final_norm),
        out_shape=jax.ShapeDtypeStruct((TOKENS, D_MODEL), F32),
        grid_spec=grid_spec,
        compiler_params=_params(("arbitrary", "arbitrary")),
        name="moe_experts",
    )(row_start, padded, hfp, addr, gw, w13, w2, x, mod,
      final_g.reshape(1, D_MODEL))


def _lambda_init(layer):
    return 0.8 - 0.6 * math.exp(-0.3 * layer)


def _split_bf16(w):
    hi = w.astype(BF16)
    return hi, (w - hi.astype(F32)).astype(BF16)


def kernel(x, c, norm_mix, norm_ffn, final_norm, ada_w, ada_b, attn_w_qkv, attn_lambda, attn_subln, attn_w_o, rec_w_in, rec_conv_w, rec_conv_b, rec_gate_w, rec_gate_b, rec_a_param, rec_w_out, moe_w_group, moe_b_group, moe_w_expert, moe_b_expert, moe_w13, moe_w2):
    xt = x.reshape(TOKENS, D_MODEL)
    mod_all = _ada_mod(c, ada_w, ada_b).reshape(DEPTH, BATCH, 6, D_MODEL)

    def lane_pad(n, *lead):
        return jnp.zeros(lead + (n,), F32)

    gap = EXPERT_ROW0 - N_GROUPS
    tail = ROUTER_LANES - EXPERT_ROW0 - N_EXPERTS
    w_router = jnp.concatenate([moe_w_group, lane_pad(gap, DEPTH, D_MODEL), moe_w_expert,
                                lane_pad(tail, DEPTH, D_MODEL)], axis=-1)
    b_router = jnp.concatenate([moe_b_group, lane_pad(gap, DEPTH), moe_b_expert,
                                lane_pad(tail, DEPTH)], axis=-1)
    wr_hi, wr_lo = _split_bf16(w_router)

    w_qkv = attn_w_qkv.astype(BF16)
    w_o = attn_w_o.astype(BF16)
    w_in = rec_w_in.astype(BF16)
    w_gate = rec_gate_w.astype(BF16)
    w_out = rec_w_out.astype(BF16)
    w13 = moe_w13.astype(BF16)
    w2 = moe_w2.astype(BF16)

    for layer in range(DEPTH):
        mod = mod_all[layer]
        j = layer // N_MIXERS
        if layer % N_MIXERS == 0:
            qkv = _norm_proj(xt, norm_mix[layer], mod, w_qkv[j], shift_row=0, scale_row=1,
                             out_scales=(Q_PRESCALE, 1.0, 1.0))
            o = _diff_attention(qkv, attn_lambda[j], attn_subln[j], _lambda_init(layer))
            xt = _proj_residual(o, w_o[j], xt, mod, gate_row=2)
        else:
            xt = _recurrent_layer(xt, norm_mix[layer], mod, w_in[j], rec_conv_w[j], rec_conv_b[j],
                                  w_gate[j], rec_gate_b[j], rec_a_param[j], w_out[j])
        hfp, ids, wts = _router(xt, norm_ffn[layer], mod, wr_hi[layer], wr_lo[layer],
                                b_router[layer].reshape(1, ROUTER_LANES))
        addr, gw, meta = _plan(ids, wts)
        xt = _moe_sparse(hfp, addr, gw, meta[:, 0, :, 0].reshape(-1), meta[:, 1, :, 0].reshape(-1),
                         w13, w2, xt, mod, final_norm,
                         layer=layer, final_norm=(layer == DEPTH - 1))

    return xt.reshape(BATCH, SEQ, D_MODEL)
```

```python
import functools
import math

import jax
import jax.numpy as jnp
from jax import lax
from jax.experimental import pallas as pl
from jax.experimental.pallas import tpu as pltpu

D_MODEL = 1024
BATCH = 32
SEQ = 2048
TOKENS = BATCH * SEQ
DEPTH = 4
N_MIXERS = 2
HEAD_DIM = 64
ATTN_HEADS = D_MODEL // (2 * HEAD_DIM)
ATTN_WIDTH = ATTN_HEADS * 2 * HEAD_DIM
SUBLN_EPS = 1e-5
D_RNN = D_MODEL
RNN_HEADS = 4
RNN_BLOCK = D_RNN // RNN_HEADS
CONV_WIDTH = 4
RG_LRU_C = 8.0
N_GROUPS = 4
EXPERTS_PER_GROUP = 4
N_EXPERTS = N_GROUPS * EXPERTS_PER_GROUP
TOP_K = 2
D_EXPERT = 512
NORM_EPS = 1e-6
Q_PRESCALE = HEAD_DIM ** -0.5 * math.log2(math.e)

LANES = 128
SUBLANES = 8
ROUTER_LANES = LANES
SCAN_PITCH_PAD = 4
VMEM_LIMIT = 56 * 1024 * 1024

F32 = jnp.float32
BF16 = jnp.bfloat16


def _params(semantics):
    return pltpu.CompilerParams(dimension_semantics=semantics, vmem_limit_bytes=VMEM_LIMIT)


def _rms_mod(x, g, shift, scale, eps=NORM_EPS):
    y = x * lax.rsqrt(jnp.mean(x * x, axis=-1, keepdims=True) + eps) * g
    return y * (1.0 + scale) + shift


def _ada_kernel(c_ref, w_ref, b_ref, o_ref):
    c = c_ref[...]
    cond = (c * jax.nn.sigmoid(c)).astype(BF16)
    o_ref[...] = jnp.dot(cond, w_ref[...].astype(BF16), preferred_element_type=F32) + b_ref[...]


def _ada_mod(c, ada_w, ada_b):
    tn = 1536
    n = 6 * D_MODEL
    return pl.pallas_call(
        _ada_kernel,
        out_shape=jax.ShapeDtypeStruct((DEPTH, BATCH, n), F32),
        grid=(DEPTH, n // tn),
        in_specs=[
            pl.BlockSpec((BATCH, D_MODEL), lambda l, j: (0, 0)),
            pl.BlockSpec((None, D_MODEL, tn), lambda l, j: (l, 0, j)),
            pl.BlockSpec((None, 1, tn), lambda l, j: (l, 0, j)),
        ],
        out_specs=pl.BlockSpec((None, BATCH, tn), lambda l, j: (l, 0, j)),
        compiler_params=_params(("arbitrary", "arbitrary")),
        name="ada_mod",
    )(c, ada_w, ada_b.reshape(DEPTH, 1, n))


def _norm_proj_kernel(x_ref, g_ref, mod_ref, w_ref, o_ref, *, shift_row, scale_row, tn, out_scales):
    h = _rms_mod(x_ref[...], g_ref[...], mod_ref[shift_row:shift_row + 1, :],
                 mod_ref[scale_row:scale_row + 1, :]).astype(BF16)
    for j in range(o_ref.shape[1] // tn):
        y = jnp.dot(h, w_ref[:, j * tn:(j + 1) * tn], preferred_element_type=F32)
        if out_scales[j] != 1.0:
            y = y * out_scales[j]
        o_ref[:, j * tn:(j + 1) * tn] = y.astype(o_ref.dtype)


def _norm_proj(x, g, mod, w, *, shift_row, scale_row, out_scales, tm=512, tn=1024):
    n = w.shape[1]
    assert len(out_scales) == n // tn
    return pl.pallas_call(
        functools.partial(_norm_proj_kernel, shift_row=shift_row, scale_row=scale_row, tn=tn,
                          out_scales=out_scales),
        out_shape=jax.ShapeDtypeStruct((TOKENS, n), BF16),
        grid=(TOKENS // tm,),
        in_specs=[
            pl.BlockSpec((tm, D_MODEL), lambda i: (i, 0)),
            pl.BlockSpec((1, D_MODEL), lambda i: (0, 0)),
            pl.BlockSpec((None, 6, D_MODEL), lambda i: (i // (SEQ // tm), 0, 0)),
            pl.BlockSpec((D_MODEL, n), lambda i: (0, 0)),
        ],
        out_specs=pl.BlockSpec((tm, n), lambda i: (i, 0)),
        compiler_params=_params(("arbitrary",)),
        name="norm_proj",
    )(x, g.reshape(1, D_MODEL), mod, w)


def _attn_kernel(lam_ref, sg_ref, q_ref, k_ref, v_ref, o_ref, qm_scr, ve_scr, m_scr, acc_scr,
                 *, tq, lambda_init):
    qi = pl.program_id(1)
    hw = 2 * HEAD_DIM
    lv = lam_ref[...]
    lam = (jnp.exp(jnp.sum(lv[0:1] * lv[1:2], keepdims=True))
           - jnp.exp(jnp.sum(lv[2:3] * lv[3:4], keepdims=True)) + lambda_init)

    lane = lax.broadcasted_iota(jnp.int32, (tq, hw), 1)
    for h in range(ATTN_HEADS):
        q = q_ref[:, h * hw:(h + 1) * hw]
        qm_scr[2 * h] = jnp.where(lane < HEAD_DIM, q, jnp.zeros_like(q))
        qm_scr[2 * h + 1] = jnp.where(lane >= HEAD_DIM, q, jnp.zeros_like(q))

    @pl.when(qi == 0)
    def _():
        ones = jnp.ones((SEQ, hw), BF16)
        for h in range(ATTN_HEADS):
            ve_scr[:, 2 * h * hw:(2 * h + 1) * hw] = v_ref[:, h * hw:(h + 1) * hw]
            ve_scr[:, (2 * h + 1) * hw:(2 * h + 2) * hw] = ones

    def step(j, masked, first):
        start = pl.multiple_of(j * tq, tq)
        if masked:
            row = lax.broadcasted_iota(jnp.int32, (tq, tq), 0)
            col = lax.broadcasted_iota(jnp.int32, (tq, tq), 1)
            keep = row >= col
        for h in range(ATTN_HEADS):
            kb = k_ref[pl.ds(start, tq), h * hw:(h + 1) * hw]
            vb = ve_scr[pl.ds(start, tq), 2 * h * hw:(2 * h + 2) * hw]
            for mi in range(2):
                c = 2 * h + mi
                s = lax.dot_general(qm_scr[c], kb, (((1,), (1,)), ((), ())),
                                    preferred_element_type=F32)
                if masked:
                    s = jnp.where(keep, s, -jnp.inf)
                m_cur = jnp.max(s, axis=-1, keepdims=True)
                if first:
                    m_new = jnp.broadcast_to(m_cur, (tq, LANES))
                else:
                    m_prev = m_scr[c]
                    m_new = jnp.maximum(m_prev, m_cur)
                    alpha = jnp.exp2(m_prev - m_new)
                p = jnp.exp2(s - jnp.concatenate([m_new] * (tq // LANES), axis=1))
                pv = jnp.dot(p.astype(BF16), vb, preferred_element_type=F32)
                if first:
                    acc_scr[c] = pv
                else:
                    acc_scr[c] = jnp.concatenate([alpha, alpha], axis=1) * acc_scr[c] + pv
                m_scr[c] = m_new

    step(qi, True, True)

    def body(j, carry):
        step(j, False, False)
        return carry

    lax.fori_loop(0, qi, body, 0)

    for h in range(ATTN_HEADS):
        a1 = acc_scr[2 * h]
        a2 = acc_scr[2 * h + 1]
        o = a1[:, :hw] / a1[:, hw:] - lam * (a2[:, :hw] / a2[:, hw:])
        o = o * lax.rsqrt(jnp.mean(o * o, axis=-1, keepdims=True) + SUBLN_EPS) * sg_ref[...]
        o_ref[:, h * hw:(h + 1) * hw] = (o * (1.0 - lambda_init)).astype(o_ref.dtype)


def _diff_attention(qkv, lam_vec, subln_g, lambda_init, *, tq=256):
    nq = SEQ // tq
    hw = 2 * HEAD_DIM
    assert hw == LANES
    chains = 2 * ATTN_HEADS
    return pl.pallas_call(
        functools.partial(_attn_kernel, tq=tq, lambda_init=lambda_init),
        out_shape=jax.ShapeDtypeStruct((TOKENS, ATTN_WIDTH), BF16),
        grid=(BATCH, nq),
        in_specs=[
            pl.BlockSpec((4, HEAD_DIM), lambda b, i: (0, 0)),
            pl.BlockSpec((1, hw), lambda b, i: (0, 0)),
            pl.BlockSpec((tq, ATTN_WIDTH), lambda b, i: (b * nq + i, 0)),
            pl.BlockSpec((SEQ, ATTN_WIDTH), lambda b, i: (b, 1)),
            pl.BlockSpec((SEQ, ATTN_WIDTH), lambda b, i: (b, 2)),
        ],
        out_specs=pl.BlockSpec((tq, ATTN_WIDTH), lambda b, i: (b * nq + i, 0)),
        scratch_shapes=[
            pltpu.VMEM((chains, tq, hw), BF16),
            pltpu.VMEM((SEQ, 2 * ATTN_WIDTH), BF16),
            pltpu.VMEM((chains, tq, LANES), F32),
            pltpu.VMEM((chains, tq, 2 * hw), F32),
        ],
        compiler_params=_params(("arbitrary", "arbitrary")),
        name="diff_attention",
    )(lam_vec, subln_g.reshape(1, hw), qkv, qkv, qkv)


def _proj_residual_kernel(a_ref, w_ref, x_ref, mod_ref, gf_ref, w_hi_ref, w_lo_ref, b_ref,
                          o_ref, hfp_ref, ids_ref, wts_ref, *, gate_row):
    y = jnp.dot(a_ref[...], w_ref[...], preferred_element_type=F32)
    x_new = x_ref[...] + mod_ref[gate_row:gate_row + 1, :] * y
    o_ref[...] = x_new
    _route_tile(x_new, gf_ref, mod_ref, w_hi_ref, w_lo_ref, b_ref, hfp_ref, ids_ref, wts_ref)


def _proj_residual(a, w, x, mod, g_ffn, wr_hi, wr_lo, b_router, *, gate_row, tm=512):
    k = a.shape[1]
    r_in, r_shape, r_out = _router_specs(tm, lambda i: i)
    return pl.pallas_call(
        functools.partial(_proj_residual_kernel, gate_row=gate_row),
        out_shape=(jax.ShapeDtypeStruct((TOKENS, D_MODEL), F32),) + r_shape,
        grid=(TOKENS // tm,),
        in_specs=[
            pl.BlockSpec((tm, k), lambda i: (i, 0)),
            pl.BlockSpec((k, D_MODEL), lambda i: (0, 0)),
            pl.BlockSpec((tm, D_MODEL), lambda i: (i, 0)),
            pl.BlockSpec((None, 6, D_MODEL), lambda i: (i // (SEQ // tm), 0, 0)),
        ] + r_in,
        out_specs=(pl.BlockSpec((tm, D_MODEL), lambda i: (i, 0)),) + r_out,
        compiler_params=_params(("arbitrary",)),
        name="proj_residual",
    )(a, w, x, mod, g_ffn.reshape(1, D_MODEL), wr_hi, wr_lo, b_router)


def _gelu_tanh(y):
    c = math.sqrt(2.0 / math.pi)
    return 0.5 * y * (1.0 + jnp.tanh(c * (y + 0.044715 * (y * y * y))))


def _rec_kernel(x_ref, g_ref, mod_ref, w_in_ref, conv_w_ref, conv_b_ref, gate_w_ref, gate_b_ref,
                a_param_ref, w_out_ref, gf_ref, w_hi_ref, w_lo_ref, b_ref,
                o_ref, hfp_ref, ids_ref, wts_ref,
                ext_scr, a_scr, b_scr, h_scr, ac_scr, carry_scr, *, tm):
    t = pl.program_id(1)
    seg = tm // SUBLANES
    pitch = seg + SCAN_PITCH_PAD
    n_lt = D_RNN // LANES

    @pl.when(t == 0)
    def _():
        carry_scr[...] = jnp.zeros(carry_scr.shape, F32)
        ext_scr[0:SUBLANES, :] = jnp.zeros((SUBLANES, D_RNN), F32)

    x = x_ref[...]
    hm = _rms_mod(x, g_ref[...], mod_ref[0:1, :], mod_ref[1:2, :]).astype(BF16)
    y = jnp.dot(hm, w_in_ref[:, :D_RNN], preferred_element_type=F32)
    xr = jnp.dot(hm, w_in_ref[:, D_RNN:], preferred_element_type=F32)

    ext_scr[SUBLANES:SUBLANES + tm, :] = xr
    xc = conv_b_ref[...] + conv_w_ref[CONV_WIDTH - 1:CONV_WIDTH, :] * xr
    for k in range(CONV_WIDTH - 1):
        back = CONV_WIDTH - 1 - k
        xc = xc + conv_w_ref[k:k + 1, :] * ext_scr[SUBLANES - back:SUBLANES - back + tm, :]
    ext_scr[0:SUBLANES, :] = xr[tm - SUBLANES:, :]

    z = -a_param_ref[...]
    softplus = jnp.maximum(z, 0.0) + jnp.log(1.0 + jnp.exp(-jnp.abs(z)))
    xcb = xc.astype(BF16)
    for hd in range(RNN_HEADS):
        sl = slice(hd * RNN_BLOCK, (hd + 1) * RNN_BLOCK)
        gates = jnp.dot(xcb[:, sl], gate_w_ref[hd], preferred_element_type=F32) + gate_b_ref[hd]
        gates = jax.nn.sigmoid(gates)
        r = gates[:, :RNN_BLOCK]
        i = gates[:, RNN_BLOCK:]
        log_a = -RG_LRU_C * r * softplus[:, sl]
        a = jnp.exp(log_a)
        b = jnp.sqrt(1.0 - a * a) * (i * xc[:, sl])
        for jj in range(RNN_BLOCK // LANES):
            lt = hd * (RNN_BLOCK // LANES) + jj
            for sg in range(SUBLANES):
                rows = slice(sg * pitch, sg * pitch + seg)
                a_scr[lt, rows, :] = a[sg * seg:(sg + 1) * seg, jj * LANES:(jj + 1) * LANES]
                b_scr[lt, rows, :] = b[sg * seg:(sg + 1) * seg, jj * LANES:(jj + 1) * LANES]

    def scan_body(v, carry):
        h, ac = carry
        step = pl.ds(v, SUBLANES, stride=pitch)
        a = jnp.concatenate([a_scr[lt, step, :] for lt in range(n_lt)], axis=1)
        b = jnp.concatenate([b_scr[lt, step, :] for lt in range(n_lt)], axis=1)
        h = a * h + b
        ac = a * ac
        for lt in range(n_lt):
            h_scr[lt, step, :] = h[:, lt * LANES:(lt + 1) * LANES]
            ac_scr[lt, step, :] = ac[:, lt * LANES:(lt + 1) * LANES]
        return h, ac

    h_end, a_end = lax.fori_loop(
        0, seg, scan_body,
        (jnp.zeros((SUBLANES, D_RNN), F32), jnp.ones((SUBLANES, D_RNN), F32)))

    sub = lax.broadcasted_iota(jnp.int32, (SUBLANES, D_RNN), 0)
    prev = carry_scr[...]
    a, b = a_end, h_end
    for d in (1, 2, 4):
        keep = sub >= d
        a_sh = jnp.where(keep, pltpu.roll(a, d, axis=0), 1.0)
        b_sh = jnp.where(keep, pltpu.roll(b, d, axis=0), 0.0)
        b = a * b_sh + b
        a = a * a_sh
    seg_end = a * prev + b
    seg_in = jnp.where(sub >= 1, pltpu.roll(seg_end, 1, axis=0), prev)
    carry_scr[...] = jnp.broadcast_to(seg_end[SUBLANES - 1:SUBLANES, :], (SUBLANES, D_RNN))

    hs = jnp.concatenate(
        [jnp.concatenate(
            [h_scr[lt, sg * pitch:sg * pitch + seg, :]
             + ac_scr[lt, sg * pitch:sg * pitch + seg, :] * seg_in[sg:sg + 1, lt * LANES:(lt + 1) * LANES]
             for lt in range(n_lt)], axis=1)
         for sg in range(SUBLANES)], axis=0)
    mixed = (_gelu_tanh(y) * hs).astype(BF16)
    out = jnp.dot(mixed, w_out_ref[...], preferred_element_type=F32)
    x_new = x + mod_ref[2:3, :] * out
    o_ref[...] = x_new
    _route_tile(x_new, gf_ref, mod_ref, w_hi_ref, w_lo_ref, b_ref, hfp_ref, ids_ref, wts_ref)


def _recurrent_layer(x, g, mod, w_in, conv_w, conv_b, gate_w, gate_b, a_param, w_out,
                     g_ffn, wr_hi, wr_lo, b_router, *, tm=512):
    nt = SEQ // tm
    const2 = lambda b, t: (0, 0)
    const3 = lambda b, t: (0, 0, 0)
    r_in, r_shape, r_out = _router_specs(tm, lambda b, t: b * nt + t)
    return pl.pallas_call(
        functools.partial(_rec_kernel, tm=tm),
        out_shape=(jax.ShapeDtypeStruct((TOKENS, D_MODEL), F32),) + r_shape,
        grid=(BATCH, nt),
        in_specs=[
            pl.BlockSpec((tm, D_MODEL), lambda b, t: (b * nt + t, 0)),
            pl.BlockSpec((1, D_MODEL), const2),
            pl.BlockSpec((None, 6, D_MODEL), lambda b, t: (b, 0, 0)),
            pl.BlockSpec((D_MODEL, 2 * D_RNN), const2),
            pl.BlockSpec((CONV_WIDTH, D_RNN), const2),
            pl.BlockSpec((1, D_RNN), const2),
            pl.BlockSpec((RNN_HEADS, RNN_BLOCK, 2 * RNN_BLOCK), const3),
            pl.BlockSpec((RNN_HEADS, 1, 2 * RNN_BLOCK), const3),
            pl.BlockSpec((1, D_RNN), const2),
            pl.BlockSpec((D_RNN, D_MODEL), const2),
        ] + r_in,
        out_specs=(pl.BlockSpec((tm, D_MODEL), lambda b, t: (b * nt + t, 0)),) + r_out,
        scratch_shapes=[
            pltpu.VMEM((tm + SUBLANES, D_RNN), F32),
            pltpu.VMEM((D_RNN // LANES, tm + SUBLANES * SUBLANES, LANES), F32),
            pltpu.VMEM((D_RNN // LANES, tm + SUBLANES * SUBLANES, LANES), F32),
            pltpu.VMEM((D_RNN // LANES, tm + SUBLANES * SUBLANES, LANES), F32),
            pltpu.VMEM((D_RNN // LANES, tm + SUBLANES * SUBLANES, LANES), F32),
            pltpu.VMEM((SUBLANES, D_RNN), F32),
        ],
        compiler_params=_params(("arbitrary", "arbitrary")),
        name="recurrent_layer",
    )(x, g.reshape(1, D_MODEL), mod, w_in, conv_w, conv_b.reshape(1, D_RNN), gate_w,
      gate_b.reshape(RNN_HEADS, 1, 2 * RNN_BLOCK), a_param.reshape(1, D_RNN), w_out,
      g_ffn.reshape(1, D_MODEL), wr_hi, wr_lo, b_router)


MOE_BLOCK = SEQ
PACK_ROWS = D_MODEL // (2 * LANES)
ROW_GRAN = 64
GRAN_SHIFT = ROW_GRAN.bit_length() - 1
TILE_STRIDE = ROW_GRAN + SUBLANES
TILE_ROWS = PACK_ROWS * TILE_STRIDE
N_TILES = TOP_K * MOE_BLOCK // ROW_GRAN + N_EXPERTS
SUPER_ROWS = 512
SUPER_SHIFT = SUPER_ROWS.bit_length() - 1
EXPERTS_PER_STEP = 2
N_EXPERT_STEPS = N_EXPERTS // EXPERTS_PER_STEP
EPI_ROWS = 512
N_EPI = MOE_BLOCK // EPI_ROWS
ROUTER_TM = 512
GROUP_ROW0 = 0
EXPERT_ROW0 = SUBLANES
HI_MASK = 0xFFFF0000


def _pack_words(lo, hi):
    lo_bits = lax.bitcast_convert_type(lo.astype(BF16).astype(F32), jnp.uint32)
    hi_bits = lax.bitcast_convert_type(hi.astype(BF16).astype(F32), jnp.uint32)
    return (hi_bits & jnp.uint32(HI_MASK)) | (lo_bits >> jnp.uint32(16))


def _unpack_words(w):
    lo = lax.bitcast_convert_type(w << jnp.uint32(16), F32)
    hi = lax.bitcast_convert_type(w & jnp.uint32(HI_MASK), F32)
    return lo, hi


def _route_tile(x, g_ref, mod_ref, w_hi_ref, w_lo_ref, b_ref, hfp_ref, ids_ref, wts_ref):
    tm = x.shape[0]
    hf = _rms_mod(x, g_ref[...], mod_ref[3:4, :], mod_ref[4:5, :])
    hi = hf.astype(BF16)
    hi32 = hi.astype(F32)
    half = D_MODEL // 2
    for j in range(PACK_ROWS):
        lo_bits = lax.bitcast_convert_type(hi32[:, j * LANES:(j + 1) * LANES], jnp.uint32)
        hi_bits = lax.bitcast_convert_type(hi32[:, half + j * LANES:half + (j + 1) * LANES],
                                           jnp.uint32)
        words = (hi_bits & jnp.uint32(HI_MASK)) | (lo_bits >> jnp.uint32(16))
        hfp_ref[pl.ds(j, tm, stride=PACK_ROWS), :] = words

    lo = (hf - hi32).astype(BF16)
    logits = (jnp.dot(hi, w_hi_ref[...], preferred_element_type=F32)
              + jnp.dot(lo, w_hi_ref[...], preferred_element_type=F32)
              + jnp.dot(hi, w_lo_ref[...], preferred_element_type=F32)) + b_ref[...]
    lt = logits.T

    neg = -jnp.inf
    row8 = lax.broadcasted_iota(jnp.int32, (SUBLANES, tm), 0)
    gl = jnp.where(row8 < N_GROUPS, lt[GROUP_ROW0:GROUP_ROW0 + SUBLANES, :], neg)
    gmax = jnp.max(gl, axis=0, keepdims=True)
    g_idx = jnp.min(jnp.where(gl == gmax, row8, SUBLANES), axis=0, keepdims=True)
    g_w = 1.0 / jnp.sum(jnp.exp(gl - gmax), axis=0, keepdims=True)

    el = lt[EXPERT_ROW0:EXPERT_ROW0 + N_EXPERTS, :]
    row16 = lax.broadcasted_iota(jnp.int32, (N_EXPERTS, tm), 0)
    emask = (row16 >> (EXPERTS_PER_GROUP.bit_length() - 1)) == g_idx
    v1 = jnp.max(jnp.where(emask, el, neg), axis=0, keepdims=True)
    i1 = jnp.min(jnp.where(emask & (el == v1), row16, N_EXPERTS), axis=0, keepdims=True)
    emask2 = emask & (row16 != i1)
    v2 = jnp.max(jnp.where(emask2, el, neg), axis=0, keepdims=True)
    i2 = jnp.min(jnp.where(emask2 & (el == v2), row16, N_EXPERTS), axis=0, keepdims=True)
    e2 = jnp.exp(v2 - v1)
    p1 = 1.0 / (1.0 + e2)
    p2 = e2 * p1
    ids_ref[...] = jnp.where(row8 == 0, i1, jnp.where(row8 == 1, i2, 0))
    wts_ref[...] = jnp.where(row8 == 0, p1 * g_w, jnp.where(row8 == 1, p2 * g_w, 0.0))


def _router_specs(tm, index):
    nt = TOKENS // tm
    assert tm == ROUTER_TM
    const = lambda *_: (0, 0)
    in_specs = [pl.BlockSpec((1, D_MODEL), const),
                pl.BlockSpec((D_MODEL, ROUTER_LANES), const),
                pl.BlockSpec((D_MODEL, ROUTER_LANES), const),
                pl.BlockSpec((1, ROUTER_LANES), const)]
    out_shape = (jax.ShapeDtypeStruct((TOKENS * PACK_ROWS, LANES), jnp.uint32),
                 jax.ShapeDtypeStruct((nt, SUBLANES, tm), jnp.int32),
                 jax.ShapeDtypeStruct((nt, SUBLANES, tm), F32))
    out_specs = (pl.BlockSpec((tm * PACK_ROWS, LANES), lambda *a: (index(*a), 0)),
                 pl.BlockSpec((None, SUBLANES, tm), lambda *a: (index(*a), 0, 0)),
                 pl.BlockSpec((None, SUBLANES, tm), lambda *a: (index(*a), 0, 0)))
    return in_specs, out_shape, out_specs


def _plan_kernel(ids_ref, wts_ref, addr_ref, gw_ref, meta_ref, rank_scr):
    nc, _, tm = ids_ref.shape
    row = lax.broadcasted_iota(jnp.int32, (N_EXPERTS, tm), 0)
    upper = (lax.broadcasted_iota(jnp.int32, (tm, tm), 0)
             < lax.broadcasted_iota(jnp.int32, (tm, tm), 1)).astype(BF16)

    def member(c):
        e0 = ids_ref[c, 0:1, :]
        e1 = ids_ref[c, 1:2, :]
        return e0, e1, ((row == e0) | (row == e1))

    counts = jnp.zeros((N_EXPERTS, 1), F32)
    for c in range(nc):
        _, _, mem = member(c)
        memf = mem.astype(F32)
        rank_scr[c] = jnp.dot(memf.astype(BF16), upper, preferred_element_type=F32) + counts
        counts = counts + jnp.sum(memf, axis=1, keepdims=True)

    tiles = jnp.floor((counts + (ROW_GRAN - 1)) * (1.0 / ROW_GRAN))
    lower = (lax.broadcasted_iota(jnp.int32, (N_EXPERTS, N_EXPERTS), 1)
             < lax.broadcasted_iota(jnp.int32, (N_EXPERTS, N_EXPERTS), 0)).astype(BF16)
    tiles_b = jnp.broadcast_to(tiles, (N_EXPERTS, LANES))
    row_start = ROW_GRAN * jnp.dot(lower, tiles_b.astype(BF16), preferred_element_type=F32)
    meta_ref[0] = row_start.astype(jnp.int32)
    meta_ref[1] = (ROW_GRAN * tiles_b).astype(jnp.int32)

    rs = row_start[:, 0:1]
    for c in range(nc):
        e0, e1, _ = member(c)
        where = rs + rank_scr[c]
        for k, ek in enumerate((e0, e1)):
            pos = jnp.sum(jnp.where(row == ek, where, 0.0), axis=0, keepdims=True).astype(jnp.int32)
            addr = (pos >> GRAN_SHIFT) * TILE_ROWS + (pos & (ROW_GRAN - 1))
            cols = slice(k * MOE_BLOCK + c * tm, k * MOE_BLOCK + (c + 1) * tm)
            addr_ref[:, cols] = addr
            gw_ref[:, cols] = wts_ref[c, k:k + 1, :]


def _plan(ids, wts):
    tm = ROUTER_TM
    nc = MOE_BLOCK // tm
    nb = TOKENS // MOE_BLOCK
    return pl.pallas_call(
        _plan_kernel,
        out_shape=(jax.ShapeDtypeStruct((nb, 1, TOP_K * MOE_BLOCK), jnp.int32),
                   jax.ShapeDtypeStruct((nb, 1, TOP_K * MOE_BLOCK), F32),
                   jax.ShapeDtypeStruct((nb, 2, N_EXPERTS, LANES), jnp.int32)),
        grid=(nb,),
        in_specs=[pl.BlockSpec((nc, SUBLANES, tm), lambda b: (b, 0, 0)),
                  pl.BlockSpec((nc, SUBLANES, tm), lambda b: (b, 0, 0))],
        out_specs=(pl.BlockSpec((None, 1, TOP_K * MOE_BLOCK), lambda b: (b, 0, 0)),
                   pl.BlockSpec((None, 1, TOP_K * MOE_BLOCK), lambda b: (b, 0, 0)),
                   pl.BlockSpec((None, 2, N_EXPERTS, LANES), lambda b: (b, 0, 0, 0))),
        scratch_shapes=[pltpu.VMEM((nc, N_EXPERTS, tm), F32)],
        compiler_params=_params(("arbitrary",)),
        name="moe_plan",
    )(ids, wts)


def _expert_rows(xy_ref, w13_ref, w2_ref, off, n_tiles):
    tile0 = lax.shift_right_logical(off, GRAN_SHIFT)
    bases = [pl.multiple_of((tile0 + ti) * TILE_ROWS, SUBLANES) for ti in range(n_tiles)]
    lo_cols, hi_cols = [], []
    for j in range(PACK_ROWS):
        words = jnp.concatenate(
            [xy_ref[pl.ds(b + j * TILE_STRIDE, ROW_GRAN), :] for b in bases], axis=0)
        lo, hi = _unpack_words(words)
        lo_cols.append(lo.astype(BF16))
        hi_cols.append(hi.astype(BF16))
    xin = jnp.concatenate(lo_cols + hi_cols, axis=1)
    u = jnp.dot(xin, w13_ref[...], preferred_element_type=F32)
    u1 = u[:, :D_EXPERT]
    act = (u1 * jax.nn.sigmoid(u1) * u[:, D_EXPERT:]).astype(BF16)
    y = jnp.dot(act, w2_ref[...], preferred_element_type=F32)
    half = D_MODEL // 2
    for j in range(PACK_ROWS):
        words = _pack_words(y[:, j * LANES:(j + 1) * LANES],
                            y[:, half + j * LANES:half + (j + 1) * LANES])
        for ti, b in enumerate(bases):
            xy_ref[pl.ds(b + j * TILE_STRIDE, ROW_GRAN), :] = words[ti * ROW_GRAN:(ti + 1) * ROW_GRAN]


def _moe_kernel(rs_ref, pad_ref, hfp_ref, addr_ref, gw_ref, w13_ref, w2_ref, x_ref, mod_ref,
                fg_ref, o_ref, xy_scr, out8_scr, *, final_norm):
    b = pl.program_id(0)
    s = pl.program_id(1)
    unroll = SUBLANES

    @pl.when((b == 0) & (s == 0))
    def _():
        xy_scr[...] = jnp.zeros(xy_scr.shape, jnp.uint32)

    @pl.when(s == 0)
    def _():
        def group(g, carry):
            for i in range(unroll):
                t = g * unroll + i
                slab = hfp_ref[pl.ds(pl.multiple_of(g * (unroll * PACK_ROWS), unroll * PACK_ROWS)
                                     + i * PACK_ROWS, PACK_ROWS), :]
                for k in range(TOP_K):
                    xy_scr[pl.ds(addr_ref[0, k * MOE_BLOCK + t], PACK_ROWS, stride=TILE_STRIDE), :] = slab
            return carry
        lax.fori_loop(0, MOE_BLOCK // unroll, group, 0)

    def run_expert(slot):
        e = b * N_EXPERTS + s * EXPERTS_PER_STEP + slot
        row0 = rs_ref[e]
        padded = pad_ref[e]

        def super_tile(i, carry):
            off = row0 + i * SUPER_ROWS
            n_tiles = lax.shift_right_logical(
                jnp.minimum(padded - i * SUPER_ROWS, SUPER_ROWS), GRAN_SHIFT)
            for m in range(1, SUPER_ROWS // ROW_GRAN + 1):
                @pl.when(n_tiles == m)
                def _(m=m):
                    _expert_rows(xy_scr, w13_ref.at[slot], w2_ref.at[slot], off, m)
            return carry
        lax.fori_loop(0, lax.shift_right_logical(padded + (SUPER_ROWS - 1), SUPER_SHIFT), super_tile, 0)

    @pl.when(s < N_EXPERT_STEPS)
    def _():
        for slot in range(EXPERTS_PER_STEP):
            run_expert(slot)

    @pl.when(s >= N_EXPERT_STEPS)
    def _():
        t0 = (s - N_EXPERT_STEPS) * EPI_ROWS

        def group(g, carry):
            for i in range(unroll):
                tl = g * unroll + i
                lo_acc = hi_acc = None
                for k in range(TOP_K):
                    pair = k * MOE_BLOCK + t0 + tl
                    words = xy_scr[pl.ds(addr_ref[0, pair], PACK_ROWS, stride=TILE_STRIDE), :]
                    lo, hi = _unpack_words(words)
                    wk = gw_ref[0, pair]
                    lo_acc = wk * lo if lo_acc is None else lo_acc + wk * lo
                    hi_acc = wk * hi if hi_acc is None else hi_acc + wk * hi
                base = pl.multiple_of(g * (unroll * SUBLANES), unroll * SUBLANES) + i * SUBLANES
                out8_scr[pl.ds(base, PACK_ROWS), :] = lo_acc
                out8_scr[pl.ds(base + PACK_ROWS, PACK_ROWS), :] = hi_acc
            return carry
        lax.fori_loop(0, EPI_ROWS // unroll, group, 0)

        n_col = D_MODEL // LANES
        for g in range(EPI_ROWS // SUBLANES):
            for j in range(n_col):
                blk = out8_scr[pl.ds(g * SUBLANES * n_col + j, SUBLANES, stride=n_col), :]
                rows = slice(g * SUBLANES, (g + 1) * SUBLANES)
                cols = slice(j * LANES, (j + 1) * LANES)
                o_ref[rows, cols] = x_ref[rows, cols] + mod_ref[5:6, cols] * blk
        if final_norm:
            v = o_ref[...]
            o_ref[...] = v * lax.rsqrt(jnp.mean(v * v, axis=-1, keepdims=True) + NORM_EPS) * fg_ref[...]


def _moe_sparse(hfp, addr, gw, row_start, padded, w13, w2, x, mod, final_g, *, layer, final_norm):
    nb = TOKENS // MOE_BLOCK
    steps = N_EXPERT_STEPS + N_EPI

    def epi_block(b, s, *_):
        return (b * N_EPI + jnp.maximum(s - N_EXPERT_STEPS, 0), 0)

    def expert_block(b, s, *_):
        return (layer, jnp.minimum(s, N_EXPERT_STEPS - 1), 0, 0)

    grid_spec = pltpu.PrefetchScalarGridSpec(
        num_scalar_prefetch=2,
        grid=(nb, steps),
        in_specs=[
            pl.BlockSpec((MOE_BLOCK * PACK_ROWS, LANES), lambda b, s, *_: (b, 0)),
            pl.BlockSpec((None, 1, TOP_K * MOE_BLOCK), lambda b, s, *_: (b, 0, 0),
                         memory_space=pltpu.SMEM),
            pl.BlockSpec((None, 1, TOP_K * MOE_BLOCK), lambda b, s, *_: (b, 0, 0),
                         memory_space=pltpu.SMEM),
            pl.BlockSpec((None, EXPERTS_PER_STEP, D_MODEL, 2 * D_EXPERT), expert_block),
            pl.BlockSpec((None, EXPERTS_PER_STEP, D_EXPERT, D_MODEL), expert_block),
            pl.BlockSpec((EPI_ROWS, D_MODEL), epi_block),
            pl.BlockSpec((None, 6, D_MODEL), lambda b, s, *_: (b, 0, 0)),
            pl.BlockSpec((1, D_MODEL), lambda b, s, *_: (0, 0)),
        ],
        out_specs=pl.BlockSpec((EPI_ROWS, D_MODEL), epi_block),
        scratch_shapes=[
            pltpu.VMEM((N_TILES * TILE_ROWS, LANES), jnp.uint32),
            pltpu.VMEM((EPI_ROWS * SUBLANES, LANES), F32),
        ],
    )
    return pl.pallas_call(
        functools.partial(_moe_kernel, final_norm=final_norm),
        out_shape=jax.ShapeDtypeStruct((TOKENS, D_MODEL), F32),
        grid_spec=grid_spec,
        compiler_params=_params(("arbitrary", "arbitrary")),
        name="moe_experts",
    )(row_start, padded, hfp, addr, gw, w13, w2, x, mod,
      final_g.reshape(1, D_MODEL))


def _lambda_init(layer):
    return 0.8 - 0.6 * math.exp(-0.3 * layer)


def _split_bf16(w):
    hi = w.astype(BF16)
    return hi, (w - hi.astype(F32)).astype(BF16)


def kernel(x, c, norm_mix, norm_ffn, final_norm, ada_w, ada_b, attn_w_qkv, attn_lambda, attn_subln, attn_w_o, rec_w_in, rec_conv_w, rec_conv_b, rec_gate_w, rec_gate_b, rec_a_param, rec_w_out, moe_w_group, moe_b_group, moe_w_expert, moe_b_expert, moe_w13, moe_w2):
    xt = x.reshape(TOKENS, D_MODEL)
    mod_all = _ada_mod(c, ada_w, ada_b).reshape(DEPTH, BATCH, 6, D_MODEL)

    def lane_pad(n, *lead):
        return jnp.zeros(lead + (n,), F32)

    gap = EXPERT_ROW0 - N_GROUPS
    tail = ROUTER_LANES - EXPERT_ROW0 - N_EXPERTS
    w_router = jnp.concatenate([moe_w_group, lane_pad(gap, DEPTH, D_MODEL), moe_w_expert,
                                lane_pad(tail, DEPTH, D_MODEL)], axis=-1)
    b_router = jnp.concatenate([moe_b_group, lane_pad(gap, DEPTH), moe_b_expert,
                                lane_pad(tail, DEPTH)], axis=-1)
    wr_hi, wr_lo = _split_bf16(w_router)

    w_qkv = attn_w_qkv.astype(BF16)
    w_o = attn_w_o.astype(BF16)
    w_in = rec_w_in.astype(BF16)
    w_gate = rec_gate_w.astype(BF16)
    w_out = rec_w_out.astype(BF16)
    w13 = moe_w13.astype(BF16)
    w2 = moe_w2.astype(BF16)

    for layer in range(DEPTH):
        mod = mod_all[layer]
        j = layer // N_MIXERS
        route = (norm_ffn[layer], wr_hi[layer], wr_lo[layer], b_router[layer].reshape(1, ROUTER_LANES))
        if layer % N_MIXERS == 0:
            qkv = _norm_proj(xt, norm_mix[layer], mod, w_qkv[j], shift_row=0, scale_row=1,
                             out_scales=(Q_PRESCALE, 1.0, 1.0))
            o = _diff_attention(qkv, attn_lambda[j], attn_subln[j], _lambda_init(layer))
            xt, hfp, ids, wts = _proj_residual(o, w_o[j], xt, mod, *route, gate_row=2)
        else:
            xt, hfp, ids, wts = _recurrent_layer(
                xt, norm_mix[layer], mod, w_in[j], rec_conv_w[j], rec_conv_b[j], w_gate[j],
                rec_gate_b[j], rec_a_param[j], w_out[j], *route)
        addr, gw, meta = _plan(ids, wts)
        xt = _moe_sparse(hfp, addr, gw, meta[:, 0, :, 0].reshape(-1), meta[:, 1, :, 0].reshape(-1),
                         w13, w2, xt, mod, final_norm,
                         layer=layer, final_norm=(layer == DEPTH - 1))

    return xt.reshape(BATCH, SEQ, D_MODEL)
```

```python
import functools
import math

import jax
import jax.numpy as jnp
from jax import lax
from jax.experimental import pallas as pl
from jax.experimental.pallas import tpu as pltpu

D_MODEL = 1024
BATCH = 32
SEQ = 2048
TOKENS = BATCH * SEQ
DEPTH = 4
N_MIXERS = 2
HEAD_DIM = 64
ATTN_HEADS = D_MODEL // (2 * HEAD_DIM)
ATTN_WIDTH = ATTN_HEADS * 2 * HEAD_DIM
SUBLN_EPS = 1e-5
D_RNN = D_MODEL
RNN_HEADS = 4
RNN_BLOCK = D_RNN // RNN_HEADS
CONV_WIDTH = 4
RG_LRU_C = 8.0
N_GROUPS = 4
EXPERTS_PER_GROUP = 4
N_EXPERTS = N_GROUPS * EXPERTS_PER_GROUP
TOP_K = 2
D_EXPERT = 512
NORM_EPS = 1e-6
Q_PRESCALE = HEAD_DIM ** -0.5 * math.log2(math.e)

LANES = 128
SUBLANES = 8
ROUTER_LANES = LANES
SCAN_PITCH_PAD = 4
V7X_VMEM_BYTES = 64 * 1024 * 1024
VMEM_LIMIT = V7X_VMEM_BYTES * 7 // 8
MIX_SHIFT, MIX_SCALE, MIX_GATE, FFN_SHIFT, FFN_SCALE, FFN_GATE = range(6)

F32 = jnp.float32
BF16 = jnp.bfloat16


def _params(semantics):
    return pltpu.CompilerParams(dimension_semantics=semantics, vmem_limit_bytes=VMEM_LIMIT)


def _row(ref, r, cols=slice(None)):
    return ref[r:r + 1, cols]


def _rms_mod(x, g, shift, scale, eps=NORM_EPS):
    y = x * lax.rsqrt(jnp.mean(x * x, axis=-1, keepdims=True) + eps) * g
    return y * (1.0 + scale) + shift


def _ada_kernel(c_ref, w_ref, b_ref, o_ref):
    c = c_ref[...]
    cond = (c * jax.nn.sigmoid(c)).astype(BF16)
    o_ref[...] = jnp.dot(cond, w_ref[...].astype(BF16), preferred_element_type=F32) + b_ref[...]


def _ada_mod(c, ada_w, ada_b):
    tn = 1536
    n = 6 * D_MODEL
    return pl.pallas_call(
        _ada_kernel,
        out_shape=jax.ShapeDtypeStruct((DEPTH, BATCH, n), F32),
        grid=(DEPTH, n // tn),
        in_specs=[
            pl.BlockSpec((BATCH, D_MODEL), lambda l, j: (0, 0)),
            pl.BlockSpec((None, D_MODEL, tn), lambda l, j: (l, 0, j)),
            pl.BlockSpec((None, 1, tn), lambda l, j: (l, 0, j)),
        ],
        out_specs=pl.BlockSpec((None, BATCH, tn), lambda l, j: (l, 0, j)),
        compiler_params=_params(("arbitrary", "arbitrary")),
        name="ada_mod",
    )(c, ada_w, ada_b.reshape(DEPTH, 1, n))


def _norm_proj_kernel(x_ref, g_ref, mod_ref, w_ref, o_ref, *, tn, out_scales):
    h = _rms_mod(x_ref[...], g_ref[...], _row(mod_ref, MIX_SHIFT), _row(mod_ref, MIX_SCALE)).astype(BF16)
    for j in range(o_ref.shape[1] // tn):
        y = jnp.dot(h, w_ref[:, j * tn:(j + 1) * tn], preferred_element_type=F32)
        if out_scales[j] != 1.0:
            y = y * out_scales[j]
        o_ref[:, j * tn:(j + 1) * tn] = y.astype(o_ref.dtype)


def _norm_proj(x, g, mod, w, *, out_scales, tm=512, tn=1024):
    n = w.shape[1]
    assert len(out_scales) == n // tn
    return pl.pallas_call(
        functools.partial(_norm_proj_kernel, tn=tn, out_scales=out_scales),
        out_shape=jax.ShapeDtypeStruct((TOKENS, n), BF16),
        grid=(TOKENS // tm,),
        in_specs=[
            pl.BlockSpec((tm, D_MODEL), lambda i: (i, 0)),
            pl.BlockSpec((1, D_MODEL), lambda i: (0, 0)),
            pl.BlockSpec((None, 6, D_MODEL), lambda i: (i // (SEQ // tm), 0, 0)),
            pl.BlockSpec((D_MODEL, n), lambda i: (0, 0)),
        ],
        out_specs=pl.BlockSpec((tm, n), lambda i: (i, 0)),
        compiler_params=_params(("arbitrary",)),
        name="norm_proj",
    )(x, g.reshape(1, D_MODEL), mod, w)


def _attn_kernel(lam_ref, sg_ref, q_ref, k_ref, v_ref, o_ref, qm_scr, ve_scr, m_scr, acc_scr,
                 *, tq, lambda_init):
    qi = pl.program_id(1)
    hw = 2 * HEAD_DIM
    lv = lam_ref[...]
    lam = (jnp.exp(jnp.sum(lv[0:1] * lv[1:2], keepdims=True))
           - jnp.exp(jnp.sum(lv[2:3] * lv[3:4], keepdims=True)) + lambda_init)

    lane = lax.broadcasted_iota(jnp.int32, (tq, hw), 1)
    for h in range(ATTN_HEADS):
        q = q_ref[:, h * hw:(h + 1) * hw]
        qm_scr[2 * h] = jnp.where(lane < HEAD_DIM, q, jnp.zeros_like(q))
        qm_scr[2 * h + 1] = jnp.where(lane >= HEAD_DIM, q, jnp.zeros_like(q))

    @pl.when(qi == 0)
    def _():
        ones = jnp.ones((SEQ, hw), BF16)
        for h in range(ATTN_HEADS):
            ve_scr[:, 2 * h * hw:(2 * h + 1) * hw] = v_ref[:, h * hw:(h + 1) * hw]
            ve_scr[:, (2 * h + 1) * hw:(2 * h + 2) * hw] = ones

    def step(j, masked, first):
        start = pl.multiple_of(j * tq, tq)
        if masked:
            row = lax.broadcasted_iota(jnp.int32, (tq, tq), 0)
            col = lax.broadcasted_iota(jnp.int32, (tq, tq), 1)
            keep = row >= col
        for h in range(ATTN_HEADS):
            kb = k_ref[pl.ds(start, tq), h * hw:(h + 1) * hw]
            vb = ve_scr[pl.ds(start, tq), 2 * h * hw:(2 * h + 2) * hw]
            for mi in range(2):
                c = 2 * h + mi
                s = lax.dot_general(qm_scr[c], kb, (((1,), (1,)), ((), ())),
                                    preferred_element_type=F32)
                if masked:
                    s = jnp.where(keep, s, -jnp.inf)
                m_cur = jnp.max(s, axis=-1, keepdims=True)
                if first:
                    m_new = jnp.broadcast_to(m_cur, (tq, LANES))
                else:
                    m_prev = m_scr[c]
                    m_new = jnp.maximum(m_prev, m_cur)
                    alpha = jnp.exp2(m_prev - m_new)
                p = jnp.exp2(s - jnp.concatenate([m_new] * (tq // LANES), axis=1))
                pv = jnp.dot(p.astype(BF16), vb, preferred_element_type=F32)
                if first:
                    acc_scr[c] = pv
                else:
                    acc_scr[c] = jnp.concatenate([alpha, alpha], axis=1) * acc_scr[c] + pv
                m_scr[c] = m_new

    step(qi, True, True)

    def body(j, carry):
        step(j, False, False)
        return carry

    lax.fori_loop(0, qi, body, 0)

    for h in range(ATTN_HEADS):
        a1 = acc_scr[2 * h]
        a2 = acc_scr[2 * h + 1]
        o = a1[:, :hw] / a1[:, hw:] - lam * (a2[:, :hw] / a2[:, hw:])
        o = o * lax.rsqrt(jnp.mean(o * o, axis=-1, keepdims=True) + SUBLN_EPS) * sg_ref[...]
        o_ref[:, h * hw:(h + 1) * hw] = (o * (1.0 - lambda_init)).astype(o_ref.dtype)


def _diff_attention(qkv, lam_vec, subln_g, lambda_init, *, tq=256):
    nq = SEQ // tq
    hw = 2 * HEAD_DIM
    assert hw == LANES
    chains = 2 * ATTN_HEADS
    return pl.pallas_call(
        functools.partial(_attn_kernel, tq=tq, lambda_init=lambda_init),
        out_shape=jax.ShapeDtypeStruct((TOKENS, ATTN_WIDTH), BF16),
        grid=(BATCH, nq),
        in_specs=[
            pl.BlockSpec((4, HEAD_DIM), lambda b, i: (0, 0)),
            pl.BlockSpec((1, hw), lambda b, i: (0, 0)),
            pl.BlockSpec((tq, ATTN_WIDTH), lambda b, i: (b * nq + i, 0)),
            pl.BlockSpec((SEQ, ATTN_WIDTH), lambda b, i: (b, 1)),
            pl.BlockSpec((SEQ, ATTN_WIDTH), lambda b, i: (b, 2)),
        ],
        out_specs=pl.BlockSpec((tq, ATTN_WIDTH), lambda b, i: (b * nq + i, 0)),
        scratch_shapes=[
            pltpu.VMEM((chains, tq, hw), BF16),
            pltpu.VMEM((SEQ, 2 * ATTN_WIDTH), BF16),
            pltpu.VMEM((chains, tq, LANES), F32),
            pltpu.VMEM((chains, tq, 2 * hw), F32),
        ],
        compiler_params=_params(("arbitrary", "arbitrary")),
        name="diff_attention",
    )(lam_vec, subln_g.reshape(1, hw), qkv, qkv, qkv)


def _proj_residual_kernel(a_ref, w_ref, x_ref, mod_ref, gf_ref, w_hi_ref, w_lo_ref, b_ref,
                          o_ref, hfp_ref, ids_ref, wts_ref):
    y = jnp.dot(a_ref[...], w_ref[...], preferred_element_type=F32)
    x_new = x_ref[...] + _row(mod_ref, MIX_GATE) * y
    o_ref[...] = x_new
    _route_tile(x_new, gf_ref, mod_ref, w_hi_ref, w_lo_ref, b_ref, hfp_ref, ids_ref, wts_ref)


def _proj_residual(a, w, x, mod, g_ffn, wr_hi, wr_lo, b_router, *, tm=512):
    k = a.shape[1]
    r_in, r_shape, r_out = _router_specs(tm, lambda i: i)
    return pl.pallas_call(
        _proj_residual_kernel,
        out_shape=(jax.ShapeDtypeStruct((TOKENS, D_MODEL), F32),) + r_shape,
        grid=(TOKENS // tm,),
        in_specs=[
            pl.BlockSpec((tm, k), lambda i: (i, 0)),
            pl.BlockSpec((k, D_MODEL), lambda i: (0, 0)),
            pl.BlockSpec((tm, D_MODEL), lambda i: (i, 0)),
            pl.BlockSpec((None, 6, D_MODEL), lambda i: (i // (SEQ // tm), 0, 0)),
        ] + r_in,
        out_specs=(pl.BlockSpec((tm, D_MODEL), lambda i: (i, 0)),) + r_out,
        compiler_params=_params(("arbitrary",)),
        name="proj_residual",
    )(a, w, x, mod, g_ffn.reshape(1, D_MODEL), wr_hi, wr_lo, b_router)


def _gelu_tanh(y):
    c = math.sqrt(2.0 / math.pi)
    return 0.5 * y * (1.0 + jnp.tanh(c * (y + 0.044715 * (y * y * y))))


def _rec_kernel(x_ref, g_ref, mod_ref, w_in_ref, conv_w_ref, conv_b_ref, gate_w_ref, gate_b_ref,
                a_param_ref, w_out_ref, gf_ref, w_hi_ref, w_lo_ref, b_ref,
                o_ref, hfp_ref, ids_ref, wts_ref,
                ext_scr, a_scr, b_scr, h_scr, ac_scr, carry_scr, *, tm):
    t = pl.program_id(1)
    seg = tm // SUBLANES
    pitch = seg + SCAN_PITCH_PAD
    n_lt = D_RNN // LANES

    @pl.when(t == 0)
    def _():
        carry_scr[...] = jnp.zeros(carry_scr.shape, F32)
        ext_scr[0:SUBLANES, :] = jnp.zeros((SUBLANES, D_RNN), F32)

    x = x_ref[...]
    hm = _rms_mod(x, g_ref[...], _row(mod_ref, MIX_SHIFT), _row(mod_ref, MIX_SCALE)).astype(BF16)
    y = jnp.dot(hm, w_in_ref[:, :D_RNN], preferred_element_type=F32)
    xr = jnp.dot(hm, w_in_ref[:, D_RNN:], preferred_element_type=F32)

    ext_scr[SUBLANES:SUBLANES + tm, :] = xr
    xc = conv_b_ref[...] + conv_w_ref[CONV_WIDTH - 1:CONV_WIDTH, :] * xr
    for k in range(CONV_WIDTH - 1):
        back = CONV_WIDTH - 1 - k
        xc = xc + conv_w_ref[k:k + 1, :] * ext_scr[SUBLANES - back:SUBLANES - back + tm, :]
    ext_scr[0:SUBLANES, :] = xr[tm - SUBLANES:, :]

    z = -a_param_ref[...]
    softplus = jnp.maximum(z, 0.0) + jnp.log(1.0 + jnp.exp(-jnp.abs(z)))
    decay = (-RG_LRU_C * math.log2(math.e)) * softplus
    xcb = xc.astype(BF16)
    for hd in range(RNN_HEADS):
        sl = slice(hd * RNN_BLOCK, (hd + 1) * RNN_BLOCK)
        gates = jnp.dot(xcb[:, sl], gate_w_ref[hd], preferred_element_type=F32) + gate_b_ref[hd]
        gates = jax.nn.sigmoid(gates)
        r = gates[:, :RNN_BLOCK]
        i = gates[:, RNN_BLOCK:]
        a = jnp.exp2(r * decay[:, sl])
        b = jnp.sqrt(1.0 - a * a) * (i * xc[:, sl])
        for jj in range(RNN_BLOCK // LANES):
            lt = hd * (RNN_BLOCK // LANES) + jj
            for sg in range(SUBLANES):
                rows = slice(sg * pitch, sg * pitch + seg)
                a_scr[lt, rows, :] = a[sg * seg:(sg + 1) * seg, jj * LANES:(jj + 1) * LANES]
                b_scr[lt, rows, :] = b[sg * seg:(sg + 1) * seg, jj * LANES:(jj + 1) * LANES]

    def scan_body(v, carry):
        h, ac = carry
        step = pl.ds(v, SUBLANES, stride=pitch)
        a = jnp.concatenate([a_scr[lt, step, :] for lt in range(n_lt)], axis=1)
        b = jnp.concatenate([b_scr[lt, step, :] for lt in range(n_lt)], axis=1)
        h = a * h + b
        ac = a * ac
        for lt in range(n_lt):
            h_scr[lt, step, :] = h[:, lt * LANES:(lt + 1) * LANES]
            ac_scr[lt, step, :] = ac[:, lt * LANES:(lt + 1) * LANES]
        return h, ac

    h_end, a_end = lax.fori_loop(
        0, seg, scan_body,
        (jnp.zeros((SUBLANES, D_RNN), F32), jnp.ones((SUBLANES, D_RNN), F32)))

    sub = lax.broadcasted_iota(jnp.int32, (SUBLANES, D_RNN), 0)
    prev = carry_scr[...]
    a, b = a_end, h_end
    for d in (1, 2, 4):
        keep = sub >= d
        a_sh = jnp.where(keep, pltpu.roll(a, d, axis=0), 1.0)
        b_sh = jnp.where(keep, pltpu.roll(b, d, axis=0), 0.0)
        b = a * b_sh + b
        a = a * a_sh
    seg_end = a * prev + b
    seg_in = jnp.where(sub >= 1, pltpu.roll(seg_end, 1, axis=0), prev)
    carry_scr[...] = jnp.broadcast_to(seg_end[SUBLANES - 1:SUBLANES, :], (SUBLANES, D_RNN))

    hs = jnp.concatenate(
        [jnp.concatenate(
            [h_scr[lt, sg * pitch:sg * pitch + seg, :]
             + ac_scr[lt, sg * pitch:sg * pitch + seg, :] * seg_in[sg:sg + 1, lt * LANES:(lt + 1) * LANES]
             for lt in range(n_lt)], axis=1)
         for sg in range(SUBLANES)], axis=0)
    mixed = (_gelu_tanh(y) * hs).astype(BF16)
    out = jnp.dot(mixed, w_out_ref[...], preferred_element_type=F32)
    x_new = x + _row(mod_ref, MIX_GATE) * out
    o_ref[...] = x_new
    _route_tile(x_new, gf_ref, mod_ref, w_hi_ref, w_lo_ref, b_ref, hfp_ref, ids_ref, wts_ref)


def _recurrent_layer(x, g, mod, w_in, conv_w, conv_b, gate_w, gate_b, a_param, w_out,
                     g_ffn, wr_hi, wr_lo, b_router, *, tm=512):
    nt = SEQ // tm
    const2 = lambda b, t: (0, 0)
    const3 = lambda b, t: (0, 0, 0)
    r_in, r_shape, r_out = _router_specs(tm, lambda b, t: b * nt + t)
    return pl.pallas_call(
        functools.partial(_rec_kernel, tm=tm),
        out_shape=(jax.ShapeDtypeStruct((TOKENS, D_MODEL), F32),) + r_shape,
        grid=(BATCH, nt),
        in_specs=[
            pl.BlockSpec((tm, D_MODEL), lambda b, t: (b * nt + t, 0)),
            pl.BlockSpec((1, D_MODEL), const2),
            pl.BlockSpec((None, 6, D_MODEL), lambda b, t: (b, 0, 0)),
            pl.BlockSpec((D_MODEL, 2 * D_RNN), const2),
            pl.BlockSpec((CONV_WIDTH, D_RNN), const2),
            pl.BlockSpec((1, D_RNN), const2),
            pl.BlockSpec((RNN_HEADS, RNN_BLOCK, 2 * RNN_BLOCK), const3),
            pl.BlockSpec((RNN_HEADS, 1, 2 * RNN_BLOCK), const3),
            pl.BlockSpec((1, D_RNN), const2),
            pl.BlockSpec((D_RNN, D_MODEL), const2),
        ] + r_in,
        out_specs=(pl.BlockSpec((tm, D_MODEL), lambda b, t: (b * nt + t, 0)),) + r_out,
        scratch_shapes=[
            pltpu.VMEM((tm + SUBLANES, D_RNN), F32),
            pltpu.VMEM((D_RNN // LANES, tm + SUBLANES * SUBLANES, LANES), F32),
            pltpu.VMEM((D_RNN // LANES, tm + SUBLANES * SUBLANES, LANES), F32),
            pltpu.VMEM((D_RNN // LANES, tm + SUBLANES * SUBLANES, LANES), F32),
            pltpu.VMEM((D_RNN // LANES, tm + SUBLANES * SUBLANES, LANES), F32),
            pltpu.VMEM((SUBLANES, D_RNN), F32),
        ],
        compiler_params=_params(("arbitrary", "arbitrary")),
        name="recurrent_layer",
    )(x, g.reshape(1, D_MODEL), mod, w_in, conv_w, conv_b.reshape(1, D_RNN), gate_w,
      gate_b.reshape(RNN_HEADS, 1, 2 * RNN_BLOCK), a_param.reshape(1, D_RNN), w_out,
      g_ffn.reshape(1, D_MODEL), wr_hi, wr_lo, b_router)


MOE_BLOCK = SEQ
PACK_ROWS = D_MODEL // (2 * LANES)
ROW_GRAN = 64
GRAN_SHIFT = ROW_GRAN.bit_length() - 1
TILE_STRIDE = ROW_GRAN + SUBLANES
TILE_ROWS = PACK_ROWS * TILE_STRIDE
N_TILES = TOP_K * MOE_BLOCK // ROW_GRAN + N_EXPERTS
SUPER_ROWS = 512
SUPER_SHIFT = SUPER_ROWS.bit_length() - 1
EXPERTS_PER_STEP = 2
N_EXPERT_STEPS = N_EXPERTS // EXPERTS_PER_STEP
EPI_ROWS = 512
N_EPI = MOE_BLOCK // EPI_ROWS
ROUTER_TM = 512
GROUP_ROW0 = 0
EXPERT_ROW0 = SUBLANES
HI_MASK = 0xFFFF0000


def _pack_words(lo, hi):
    lo_bits = lax.bitcast_convert_type(lo.astype(BF16).astype(F32), jnp.uint32)
    hi_bits = lax.bitcast_convert_type(hi.astype(BF16).astype(F32), jnp.uint32)
    return (hi_bits & jnp.uint32(HI_MASK)) | (lo_bits >> jnp.uint32(16))


def _unpack_words(w):
    lo = lax.bitcast_convert_type(w << jnp.uint32(16), F32)
    hi = lax.bitcast_convert_type(w & jnp.uint32(HI_MASK), F32)
    return lo, hi


def _route_tile(x, g_ref, mod_ref, w_hi_ref, w_lo_ref, b_ref, hfp_ref, ids_ref, wts_ref):
    tm = x.shape[0]
    hf = _rms_mod(x, g_ref[...], _row(mod_ref, FFN_SHIFT), _row(mod_ref, FFN_SCALE))
    hi = hf.astype(BF16)
    hi32 = hi.astype(F32)
    half = D_MODEL // 2
    for j in range(PACK_ROWS):
        lo_bits = lax.bitcast_convert_type(hi32[:, j * LANES:(j + 1) * LANES], jnp.uint32)
        hi_bits = lax.bitcast_convert_type(hi32[:, half + j * LANES:half + (j + 1) * LANES],
                                           jnp.uint32)
        words = (hi_bits & jnp.uint32(HI_MASK)) | (lo_bits >> jnp.uint32(16))
        hfp_ref[pl.ds(j, tm, stride=PACK_ROWS), :] = words

    lo = (hf - hi32).astype(BF16)
    logits = (jnp.dot(hi, w_hi_ref[...], preferred_element_type=F32)
              + jnp.dot(lo, w_hi_ref[...], preferred_element_type=F32)
              + jnp.dot(hi, w_lo_ref[...], preferred_element_type=F32)) + b_ref[...]
    lt = logits.T

    neg = -jnp.inf
    row8 = lax.broadcasted_iota(jnp.int32, (SUBLANES, tm), 0)
    gl = jnp.where(row8 < N_GROUPS, lt[GROUP_ROW0:GROUP_ROW0 + SUBLANES, :], neg)
    gmax = jnp.max(gl, axis=0, keepdims=True)
    g_idx = jnp.min(jnp.where(gl == gmax, row8, SUBLANES), axis=0, keepdims=True)
    g_w = 1.0 / jnp.sum(jnp.exp(gl - gmax), axis=0, keepdims=True)

    el = lt[EXPERT_ROW0:EXPERT_ROW0 + N_EXPERTS, :]
    row16 = lax.broadcasted_iota(jnp.int32, (N_EXPERTS, tm), 0)
    emask = (row16 >> (EXPERTS_PER_GROUP.bit_length() - 1)) == g_idx
    v1 = jnp.max(jnp.where(emask, el, neg), axis=0, keepdims=True)
    i1 = jnp.min(jnp.where(emask & (el == v1), row16, N_EXPERTS), axis=0, keepdims=True)
    emask2 = emask & (row16 != i1)
    v2 = jnp.max(jnp.where(emask2, el, neg), axis=0, keepdims=True)
    i2 = jnp.min(jnp.where(emask2 & (el == v2), row16, N_EXPERTS), axis=0, keepdims=True)
    e2 = jnp.exp(v2 - v1)
    p1 = 1.0 / (1.0 + e2)
    p2 = e2 * p1
    ids_ref[...] = jnp.where(row8 == 0, i1, jnp.where(row8 == 1, i2, 0))
    wts_ref[...] = jnp.where(row8 == 0, p1 * g_w, jnp.where(row8 == 1, p2 * g_w, 0.0))


def _router_specs(tm, index):
    nt = TOKENS // tm
    assert tm == ROUTER_TM
    const = lambda *_: (0, 0)
    in_specs = [pl.BlockSpec((1, D_MODEL), const),
                pl.BlockSpec((D_MODEL, ROUTER_LANES), const),
                pl.BlockSpec((D_MODEL, ROUTER_LANES), const),
                pl.BlockSpec((1, ROUTER_LANES), const)]
    out_shape = (jax.ShapeDtypeStruct((TOKENS * PACK_ROWS, LANES), jnp.uint32),
                 jax.ShapeDtypeStruct((nt, SUBLANES, tm), jnp.int32),
                 jax.ShapeDtypeStruct((nt, SUBLANES, tm), F32))
    out_specs = (pl.BlockSpec((tm * PACK_ROWS, LANES), lambda *a: (index(*a), 0)),
                 pl.BlockSpec((None, SUBLANES, tm), lambda *a: (index(*a), 0, 0)),
                 pl.BlockSpec((None, SUBLANES, tm), lambda *a: (index(*a), 0, 0)))
    return in_specs, out_shape, out_specs


def _plan_kernel(ids_ref, wts_ref, addr_ref, gw_ref, meta_ref, rank_scr):
    nc, _, tm = ids_ref.shape
    row = lax.broadcasted_iota(jnp.int32, (N_EXPERTS, tm), 0)
    upper = (lax.broadcasted_iota(jnp.int32, (tm, tm), 0)
             < lax.broadcasted_iota(jnp.int32, (tm, tm), 1)).astype(BF16)

    def member(c):
        e0 = ids_ref[c, 0:1, :]
        e1 = ids_ref[c, 1:2, :]
        return e0, e1, ((row == e0) | (row == e1))

    counts = jnp.zeros((N_EXPERTS, 1), F32)
    for c in range(nc):
        _, _, mem = member(c)
        memf = mem.astype(F32)
        rank_scr[c] = jnp.dot(memf.astype(BF16), upper, preferred_element_type=F32) + counts
        counts = counts + jnp.sum(memf, axis=1, keepdims=True)

    tiles = jnp.floor((counts + (ROW_GRAN - 1)) * (1.0 / ROW_GRAN))
    lower = (lax.broadcasted_iota(jnp.int32, (N_EXPERTS, N_EXPERTS), 1)
             < lax.broadcasted_iota(jnp.int32, (N_EXPERTS, N_EXPERTS), 0)).astype(BF16)
    tiles_b = jnp.broadcast_to(tiles, (N_EXPERTS, LANES))
    row_start = ROW_GRAN * jnp.dot(lower, tiles_b.astype(BF16), preferred_element_type=F32)
    meta_ref[0] = row_start.astype(jnp.int32)
    meta_ref[1] = (ROW_GRAN * tiles_b).astype(jnp.int32)

    rs = row_start[:, 0:1]
    for c in range(nc):
        e0, e1, _ = member(c)
        where = rs + rank_scr[c]
        for k, ek in enumerate((e0, e1)):
            pos = jnp.sum(jnp.where(row == ek, where, 0.0), axis=0, keepdims=True).astype(jnp.int32)
            addr = (pos >> GRAN_SHIFT) * TILE_ROWS + (pos & (ROW_GRAN - 1))
            cols = slice(k * MOE_BLOCK + c * tm, k * MOE_BLOCK + (c + 1) * tm)
            addr_ref[:, cols] = addr
            gw_ref[:, cols] = wts_ref[c, k:k + 1, :]


def _plan(ids, wts):
    tm = ROUTER_TM
    nc = MOE_BLOCK // tm
    nb = TOKENS // MOE_BLOCK
    return pl.pallas_call(
        _plan_kernel,
        out_shape=(jax.ShapeDtypeStruct((nb, 1, TOP_K * MOE_BLOCK), jnp.int32),
                   jax.ShapeDtypeStruct((nb, 1, TOP_K * MOE_BLOCK), F32),
                   jax.ShapeDtypeStruct((nb, 2, N_EXPERTS, LANES), jnp.int32)),
        grid=(nb,),
        in_specs=[pl.BlockSpec((nc, SUBLANES, tm), lambda b: (b, 0, 0)),
                  pl.BlockSpec((nc, SUBLANES, tm), lambda b: (b, 0, 0))],
        out_specs=(pl.BlockSpec((None, 1, TOP_K * MOE_BLOCK), lambda b: (b, 0, 0)),
                   pl.BlockSpec((None, 1, TOP_K * MOE_BLOCK), lambda b: (b, 0, 0)),
                   pl.BlockSpec((None, 2, N_EXPERTS, LANES), lambda b: (b, 0, 0, 0))),
        scratch_shapes=[pltpu.VMEM((nc, N_EXPERTS, tm), F32)],
        compiler_params=_params(("arbitrary",)),
        name="moe_plan",
    )(ids, wts)


def _expert_rows(xy_ref, w13_ref, w2_ref, off, n_tiles):
    tile0 = lax.shift_right_logical(off, GRAN_SHIFT)
    bases = [pl.multiple_of((tile0 + ti) * TILE_ROWS, SUBLANES) for ti in range(n_tiles)]
    lo_cols, hi_cols = [], []
    for j in range(PACK_ROWS):
        words = jnp.concatenate(
            [xy_ref[pl.ds(b + j * TILE_STRIDE, ROW_GRAN), :] for b in bases], axis=0)
        lo, hi = _unpack_words(words)
        lo_cols.append(lo.astype(BF16))
        hi_cols.append(hi.astype(BF16))
    xin = jnp.concatenate(lo_cols + hi_cols, axis=1)
    u = jnp.dot(xin, w13_ref[...], preferred_element_type=F32)
    u1 = u[:, :D_EXPERT]
    act = (u1 * jax.nn.sigmoid(u1) * u[:, D_EXPERT:]).astype(BF16)
    y = jnp.dot(act, w2_ref[...], preferred_element_type=F32)
    half = D_MODEL // 2
    for j in range(PACK_ROWS):
        words = _pack_words(y[:, j * LANES:(j + 1) * LANES],
                            y[:, half + j * LANES:half + (j + 1) * LANES])
        for ti, b in enumerate(bases):
            xy_ref[pl.ds(b + j * TILE_STRIDE, ROW_GRAN), :] = words[ti * ROW_GRAN:(ti + 1) * ROW_GRAN]


def _moe_kernel(rs_ref, pad_ref, hfp_ref, addr_ref, gw_ref, w13_ref, w2_ref, x_ref, mod_ref,
                fg_ref, o_ref, xy_scr, out8_scr, *, final_norm):
    b = pl.program_id(0)
    s = pl.program_id(1)
    unroll = SUBLANES

    @pl.when((b == 0) & (s == 0))
    def _():
        xy_scr[...] = jnp.zeros(xy_scr.shape, jnp.uint32)

    @pl.when(s == 0)
    def _():
        def group(g, carry):
            for i in range(unroll):
                t = g * unroll + i
                slab = hfp_ref[pl.ds(pl.multiple_of(g * (unroll * PACK_ROWS), unroll * PACK_ROWS)
                                     + i * PACK_ROWS, PACK_ROWS), :]
                for k in range(TOP_K):
                    xy_scr[pl.ds(addr_ref[0, k * MOE_BLOCK + t], PACK_ROWS, stride=TILE_STRIDE), :] = slab
            return carry
        lax.fori_loop(0, MOE_BLOCK // unroll, group, 0)

    def run_expert(slot):
        e = b * N_EXPERTS + s * EXPERTS_PER_STEP + slot
        row0 = rs_ref[e]
        padded = pad_ref[e]

        def super_tile(i, carry):
            off = row0 + i * SUPER_ROWS
            n_tiles = lax.shift_right_logical(
                jnp.minimum(padded - i * SUPER_ROWS, SUPER_ROWS), GRAN_SHIFT)
            for m in range(1, SUPER_ROWS // ROW_GRAN + 1):
                @pl.when(n_tiles == m)
                def _(m=m):
                    _expert_rows(xy_scr, w13_ref.at[slot], w2_ref.at[slot], off, m)
            return carry
        lax.fori_loop(0, lax.shift_right_logical(padded + (SUPER_ROWS - 1), SUPER_SHIFT), super_tile, 0)

    @pl.when(s < N_EXPERT_STEPS)
    def _():
        for slot in range(EXPERTS_PER_STEP):
            run_expert(slot)

    @pl.when(s >= N_EXPERT_STEPS)
    def _():
        t0 = (s - N_EXPERT_STEPS) * EPI_ROWS

        def group(g, carry):
            for i in range(unroll):
                tl = g * unroll + i
                lo_acc = hi_acc = None
                for k in range(TOP_K):
                    pair = k * MOE_BLOCK + t0 + tl
                    words = xy_scr[pl.ds(addr_ref[0, pair], PACK_ROWS, stride=TILE_STRIDE), :]
                    lo, hi = _unpack_words(words)
                    wk = gw_ref[0, pair]
                    lo_acc = wk * lo if lo_acc is None else lo_acc + wk * lo
                    hi_acc = wk * hi if hi_acc is None else hi_acc + wk * hi
                base = pl.multiple_of(g * (unroll * SUBLANES), unroll * SUBLANES) + i * SUBLANES
                out8_scr[pl.ds(base, PACK_ROWS), :] = lo_acc
                out8_scr[pl.ds(base + PACK_ROWS, PACK_ROWS), :] = hi_acc
            return carry
        lax.fori_loop(0, EPI_ROWS // unroll, group, 0)

        n_col = D_MODEL // LANES
        for g in range(EPI_ROWS // SUBLANES):
            for j in range(n_col):
                blk = out8_scr[pl.ds(g * SUBLANES * n_col + j, SUBLANES, stride=n_col), :]
                rows = slice(g * SUBLANES, (g + 1) * SUBLANES)
                cols = slice(j * LANES, (j + 1) * LANES)
                o_ref[rows, cols] = x_ref[rows, cols] + _row(mod_ref, FFN_GATE, cols) * blk
        if final_norm:
            v = o_ref[...]
            o_ref[...] = v * lax.rsqrt(jnp.mean(v * v, axis=-1, keepdims=True) + NORM_EPS) * fg_ref[...]


def _moe_sparse(hfp, addr, gw, row_start, padded, w13, w2, x, mod, final_g, *, layer, final_norm):
    nb = TOKENS // MOE_BLOCK
    steps = N_EXPERT_STEPS + N_EPI

    def epi_block(b, s, *_):
        return (b * N_EPI + jnp.maximum(s - N_EXPERT_STEPS, 0), 0)

    def expert_block(b, s, *_):
        return (layer, jnp.minimum(s, N_EXPERT_STEPS - 1), 0, 0)

    grid_spec = pltpu.PrefetchScalarGridSpec(
        num_scalar_prefetch=2,
        grid=(nb, steps),
        in_specs=[
            pl.BlockSpec((MOE_BLOCK * PACK_ROWS, LANES), lambda b, s, *_: (b, 0)),
            pl.BlockSpec((None, 1, TOP_K * MOE_BLOCK), lambda b, s, *_: (b, 0, 0),
                         memory_space=pltpu.SMEM),
            pl.BlockSpec((None, 1, TOP_K * MOE_BLOCK), lambda b, s, *_: (b, 0, 0),
                         memory_space=pltpu.SMEM),
            pl.BlockSpec((None, EXPERTS_PER_STEP, D_MODEL, 2 * D_EXPERT), expert_block),
            pl.BlockSpec((None, EXPERTS_PER_STEP, D_EXPERT, D_MODEL), expert_block),
            pl.BlockSpec((EPI_ROWS, D_MODEL), epi_block),
            pl.BlockSpec((None, 6, D_MODEL), lambda b, s, *_: (b, 0, 0)),
            pl.BlockSpec((1, D_MODEL), lambda b, s, *_: (0, 0)),
        ],
        out_specs=pl.BlockSpec((EPI_ROWS, D_MODEL), epi_block),
        scratch_shapes=[
            pltpu.VMEM((N_TILES * TILE_ROWS, LANES), jnp.uint32),
            pltpu.VMEM((EPI_ROWS * SUBLANES, LANES), F32),
        ],
    )
    return pl.pallas_call(
        functools.partial(_moe_kernel, final_norm=final_norm),
        out_shape=jax.ShapeDtypeStruct((TOKENS, D_MODEL), F32),
        grid_spec=grid_spec,
        compiler_params=_params(("arbitrary", "arbitrary")),
        name="moe_experts",
    )(row_start, padded, hfp, addr, gw, w13, w2, x, mod,
      final_g.reshape(1, D_MODEL))


def _lambda_init(layer):
    return 0.8 - 0.6 * math.exp(-0.3 * layer)


def _split_bf16(w):
    hi = w.astype(BF16)
    return hi, (w - hi.astype(F32)).astype(BF16)


def kernel(x, c, norm_mix, norm_ffn, final_norm, ada_w, ada_b, attn_w_qkv, attn_lambda, attn_subln, attn_w_o, rec_w_in, rec_conv_w, rec_conv_b, rec_gate_w, rec_gate_b, rec_a_param, rec_w_out, moe_w_group, moe_b_group, moe_w_expert, moe_b_expert, moe_w13, moe_w2):
    xt = x.reshape(TOKENS, D_MODEL)
    mod_all = _ada_mod(c, ada_w, ada_b).reshape(DEPTH, BATCH, 6, D_MODEL)

    def lane_pad(n, *lead):
        return jnp.zeros(lead + (n,), F32)

    gap = EXPERT_ROW0 - N_GROUPS
    tail = ROUTER_LANES - EXPERT_ROW0 - N_EXPERTS
    w_router = jnp.concatenate([moe_w_group, lane_pad(gap, DEPTH, D_MODEL), moe_w_expert,
                                lane_pad(tail, DEPTH, D_MODEL)], axis=-1)
    b_router = jnp.concatenate([moe_b_group, lane_pad(gap, DEPTH), moe_b_expert,
                                lane_pad(tail, DEPTH)], axis=-1)
    wr_hi, wr_lo = _split_bf16(w_router)

    w_qkv = attn_w_qkv.astype(BF16)
    w_o = attn_w_o.astype(BF16)
    w_in = rec_w_in.astype(BF16)
    w_gate = rec_gate_w.astype(BF16)
    w_out = rec_w_out.astype(BF16)
    w13 = moe_w13.astype(BF16)
    w2 = moe_w2.astype(BF16)

    for layer in range(DEPTH):
        mod = mod_all[layer]
        j = layer // N_MIXERS
        route = (norm_ffn[layer], wr_hi[layer], wr_lo[layer], b_router[layer].reshape(1, ROUTER_LANES))
        if layer % N_MIXERS == 0:
            qkv = _norm_proj(xt, norm_mix[layer], mod, w_qkv[j], out_scales=(Q_PRESCALE, 1.0, 1.0))
            o = _diff_attention(qkv, attn_lambda[j], attn_subln[j], _lambda_init(layer))
            xt, hfp, ids, wts = _proj_residual(o, w_o[j], xt, mod, *route)
        else:
            xt, hfp, ids, wts = _recurrent_layer(
                xt, norm_mix[layer], mod, w_in[j], rec_conv_w[j], rec_conv_b[j], w_gate[j],
                rec_gate_b[j], rec_a_param[j], w_out[j], *route)
        addr, gw, meta = _plan(ids, wts)
        xt = _moe_sparse(hfp, addr, gw, meta[:, 0, :, 0].reshape(-1), meta[:, 1, :, 0].reshape(-1),
                         w13, w2, xt, mod, final_norm,
                         layer=layer, final_norm=(layer == DEPTH - 1))

    return xt.reshape(BATCH, SEQ, D_MODEL)
```

```python
import functools
import math

import jax
import jax.numpy as jnp
from jax import lax
from jax.experimental import pallas as pl
from jax.experimental.pallas import tpu as pltpu

D_MODEL = 1024
BATCH = 32
SEQ = 2048
TOKENS = BATCH * SEQ
DEPTH = 4
N_MIXERS = 2
HEAD_DIM = 64
ATTN_HEADS = D_MODEL // (2 * HEAD_DIM)
ATTN_WIDTH = ATTN_HEADS * 2 * HEAD_DIM
SUBLN_EPS = 1e-5
D_RNN = D_MODEL
RNN_HEADS = 4
RNN_BLOCK = D_RNN // RNN_HEADS
CONV_WIDTH = 4
RG_LRU_C = 8.0
N_GROUPS = 4
EXPERTS_PER_GROUP = 4
N_EXPERTS = N_GROUPS * EXPERTS_PER_GROUP
TOP_K = 2
D_EXPERT = 512
NORM_EPS = 1e-6
Q_PRESCALE = HEAD_DIM ** -0.5 * math.log2(math.e)

LANES = 128
SUBLANES = 8
ROUTER_LANES = LANES
SCAN_PITCH_PAD = 4
V7X_VMEM_BYTES = 64 * 1024 * 1024
VMEM_LIMIT = V7X_VMEM_BYTES * 7 // 8
MIX_SHIFT, MIX_SCALE, MIX_GATE, FFN_SHIFT, FFN_SCALE, FFN_GATE = range(6)

F32 = jnp.float32
BF16 = jnp.bfloat16


def _params(semantics):
    return pltpu.CompilerParams(dimension_semantics=semantics, vmem_limit_bytes=VMEM_LIMIT)


def _row(ref, r, cols=slice(None)):
    return ref[r:r + 1, cols]


def _rms_mod(x, g, shift, scale, eps=NORM_EPS):
    y = x * lax.rsqrt(jnp.mean(x * x, axis=-1, keepdims=True) + eps) * g
    return y * (1.0 + scale) + shift


def _ada_kernel(c_ref, w_ref, b_ref, o_ref):
    c = c_ref[...]
    cond = (c * jax.nn.sigmoid(c)).astype(BF16)
    o_ref[...] = jnp.dot(cond, w_ref[...].astype(BF16), preferred_element_type=F32) + b_ref[...]


def _ada_mod(c, ada_w, ada_b):
    tn = 1536
    n = 6 * D_MODEL
    return pl.pallas_call(
        _ada_kernel,
        out_shape=jax.ShapeDtypeStruct((DEPTH, BATCH, n), F32),
        grid=(DEPTH, n // tn),
        in_specs=[
            pl.BlockSpec((BATCH, D_MODEL), lambda l, j: (0, 0)),
            pl.BlockSpec((None, D_MODEL, tn), lambda l, j: (l, 0, j)),
            pl.BlockSpec((None, 1, tn), lambda l, j: (l, 0, j)),
        ],
        out_specs=pl.BlockSpec((None, BATCH, tn), lambda l, j: (l, 0, j)),
        compiler_params=_params(("arbitrary", "arbitrary")),
        name="ada_mod",
    )(c, ada_w, ada_b.reshape(DEPTH, 1, n))


def _norm_proj_kernel(x_ref, g_ref, mod_ref, w_ref, o_ref, *, tn, out_scales):
    h = _rms_mod(x_ref[...], g_ref[...], _row(mod_ref, MIX_SHIFT), _row(mod_ref, MIX_SCALE)).astype(BF16)
    for j in range(o_ref.shape[1] // tn):
        y = jnp.dot(h, w_ref[:, j * tn:(j + 1) * tn], preferred_element_type=F32)
        if out_scales[j] != 1.0:
            y = y * out_scales[j]
        o_ref[:, j * tn:(j + 1) * tn] = y.astype(o_ref.dtype)


def _norm_proj(x, g, mod, w, *, out_scales, tm=512, tn=1024):
    n = w.shape[1]
    assert len(out_scales) == n // tn
    return pl.pallas_call(
        functools.partial(_norm_proj_kernel, tn=tn, out_scales=out_scales),
        out_shape=jax.ShapeDtypeStruct((TOKENS, n), BF16),
        grid=(TOKENS // tm,),
        in_specs=[
            pl.BlockSpec((tm, D_MODEL), lambda i: (i, 0)),
            pl.BlockSpec((1, D_MODEL), lambda i: (0, 0)),
            pl.BlockSpec((None, 6, D_MODEL), lambda i: (i // (SEQ // tm), 0, 0)),
            pl.BlockSpec((D_MODEL, n), lambda i: (0, 0)),
        ],
        out_specs=pl.BlockSpec((tm, n), lambda i: (i, 0)),
        compiler_params=_params(("arbitrary",)),
        name="norm_proj",
    )(x, g.reshape(1, D_MODEL), mod, w)


def _attn_kernel(lam_ref, sg_ref, q_ref, k_ref, v_ref, o_ref, qm_scr, ve_scr, m_scr, acc_scr,
                 *, tq, lambda_init):
    qi = pl.program_id(1)
    hw = 2 * HEAD_DIM
    lv = lam_ref[...]
    lam = (jnp.exp(jnp.sum(lv[0:1] * lv[1:2], keepdims=True))
           - jnp.exp(jnp.sum(lv[2:3] * lv[3:4], keepdims=True)) + lambda_init)

    lane = lax.broadcasted_iota(jnp.int32, (tq, hw), 1)
    for h in range(ATTN_HEADS):
        q = q_ref[:, h * hw:(h + 1) * hw]
        qm_scr[2 * h] = jnp.where(lane < HEAD_DIM, q, jnp.zeros_like(q))
        qm_scr[2 * h + 1] = jnp.where(lane >= HEAD_DIM, q, jnp.zeros_like(q))

    @pl.when(qi == 0)
    def _():
        ones = jnp.ones((SEQ, hw), BF16)
        for h in range(ATTN_HEADS):
            ve_scr[:, 2 * h * hw:(2 * h + 1) * hw] = v_ref[:, h * hw:(h + 1) * hw]
            ve_scr[:, (2 * h + 1) * hw:(2 * h + 2) * hw] = ones

    def step(j, masked, first):
        start = pl.multiple_of(j * tq, tq)
        if masked:
            row = lax.broadcasted_iota(jnp.int32, (tq, tq), 0)
            col = lax.broadcasted_iota(jnp.int32, (tq, tq), 1)
            keep = row >= col
        for h in range(ATTN_HEADS):
            kb = k_ref[pl.ds(start, tq), h * hw:(h + 1) * hw]
            vb = ve_scr[pl.ds(start, tq), 2 * h * hw:(2 * h + 2) * hw]
            for mi in range(2):
                c = 2 * h + mi
                s = lax.dot_general(qm_scr[c], kb, (((1,), (1,)), ((), ())),
                                    preferred_element_type=F32)
                if masked:
                    s = jnp.where(keep, s, -jnp.inf)
                m_cur = jnp.max(s, axis=-1, keepdims=True)
                if first:
                    m_new = jnp.broadcast_to(m_cur, (tq, LANES))
                else:
                    m_prev = m_scr[c]
                    m_new = jnp.maximum(m_prev, m_cur)
                    alpha = jnp.exp2(m_prev - m_new)
                p = jnp.exp2(s - jnp.concatenate([m_new] * (tq // LANES), axis=1))
                pv = jnp.dot(p.astype(BF16), vb, preferred_element_type=F32)
                if first:
                    acc_scr[c] = pv
                else:
                    acc_scr[c] = jnp.concatenate([alpha, alpha], axis=1) * acc_scr[c] + pv
                m_scr[c] = m_new

    step(qi, True, True)

    def body(j, carry):
        step(j, False, False)
        return carry

    lax.fori_loop(0, qi, body, 0)

    for h in range(ATTN_HEADS):
        a1 = acc_scr[2 * h]
        a2 = acc_scr[2 * h + 1]
        o = a1[:, :hw] / a1[:, hw:] - lam * (a2[:, :hw] / a2[:, hw:])
        o = o * lax.rsqrt(jnp.mean(o * o, axis=-1, keepdims=True) + SUBLN_EPS) * sg_ref[...]
        o_ref[:, h * hw:(h + 1) * hw] = (o * (1.0 - lambda_init)).astype(o_ref.dtype)


def _diff_attention(qkv, lam_vec, subln_g, lambda_init, *, tq=256):
    nq = SEQ // tq
    hw = 2 * HEAD_DIM
    assert hw == LANES
    chains = 2 * ATTN_HEADS
    return pl.pallas_call(
        functools.partial(_attn_kernel, tq=tq, lambda_init=lambda_init),
        out_shape=jax.ShapeDtypeStruct((TOKENS, ATTN_WIDTH), BF16),
        grid=(BATCH, nq),
        in_specs=[
            pl.BlockSpec((4, HEAD_DIM), lambda b, i: (0, 0)),
            pl.BlockSpec((1, hw), lambda b, i: (0, 0)),
            pl.BlockSpec((tq, ATTN_WIDTH), lambda b, i: (b * nq + i, 0)),
            pl.BlockSpec((SEQ, ATTN_WIDTH), lambda b, i: (b, 1)),
            pl.BlockSpec((SEQ, ATTN_WIDTH), lambda b, i: (b, 2)),
        ],
        out_specs=pl.BlockSpec((tq, ATTN_WIDTH), lambda b, i: (b * nq + i, 0)),
        scratch_shapes=[
            pltpu.VMEM((chains, tq, hw), BF16),
            pltpu.VMEM((SEQ, 2 * ATTN_WIDTH), BF16),
            pltpu.VMEM((chains, tq, LANES), F32),
            pltpu.VMEM((chains, tq, 2 * hw), F32),
        ],
        compiler_params=_params(("arbitrary", "arbitrary")),
        name="diff_attention",
    )(lam_vec, subln_g.reshape(1, hw), qkv, qkv, qkv)


def _proj_residual_kernel(a_ref, w_ref, x_ref, mod_ref, gf_ref, w_hi_ref, w_lo_ref, b_ref,
                          o_ref, hfp_ref, ids_ref, wts_ref):
    y = jnp.dot(a_ref[...], w_ref[...], preferred_element_type=F32)
    x_new = x_ref[...] + _row(mod_ref, MIX_GATE) * y
    o_ref[...] = x_new
    _route_tile(x_new, gf_ref, mod_ref, w_hi_ref, w_lo_ref, b_ref, hfp_ref, ids_ref, wts_ref)


def _proj_residual(a, w, x, mod, g_ffn, wr_hi, wr_lo, b_router, *, tm=512):
    k = a.shape[1]
    r_in, r_shape, r_out = _router_specs(tm, lambda i: i)
    return pl.pallas_call(
        _proj_residual_kernel,
        out_shape=(jax.ShapeDtypeStruct((TOKENS, D_MODEL), F32),) + r_shape,
        grid=(TOKENS // tm,),
        in_specs=[
            pl.BlockSpec((tm, k), lambda i: (i, 0)),
            pl.BlockSpec((k, D_MODEL), lambda i: (0, 0)),
            pl.BlockSpec((tm, D_MODEL), lambda i: (i, 0)),
            pl.BlockSpec((None, 6, D_MODEL), lambda i: (i // (SEQ // tm), 0, 0)),
        ] + r_in,
        out_specs=(pl.BlockSpec((tm, D_MODEL), lambda i: (i, 0)),) + r_out,
        compiler_params=_params(("arbitrary",)),
        name="proj_residual",
    )(a, w, x, mod, g_ffn.reshape(1, D_MODEL), wr_hi, wr_lo, b_router)


def _gelu_tanh(y):
    c = math.sqrt(2.0 / math.pi)
    return 0.5 * y * (1.0 + jnp.tanh(c * (y + 0.044715 * (y * y * y))))


def _rec_kernel(x_ref, g_ref, mod_ref, w_in_ref, conv_w_ref, conv_b_ref, gate_w_ref, gate_b_ref,
                a_param_ref, w_out_ref, gf_ref, w_hi_ref, w_lo_ref, b_ref,
                o_ref, hfp_ref, ids_ref, wts_ref,
                ext_scr, a_scr, b_scr, h_scr, ac_scr, carry_scr, *, tm):
    t = pl.program_id(1)
    seg = tm // SUBLANES
    pitch = seg + SCAN_PITCH_PAD
    n_lt = D_RNN // LANES

    @pl.when(t == 0)
    def _():
        carry_scr[...] = jnp.zeros(carry_scr.shape, F32)
        ext_scr[0:SUBLANES, :] = jnp.zeros((SUBLANES, D_RNN), F32)

    x = x_ref[...]
    hm = _rms_mod(x, g_ref[...], _row(mod_ref, MIX_SHIFT), _row(mod_ref, MIX_SCALE)).astype(BF16)
    y = jnp.dot(hm, w_in_ref[:, :D_RNN], preferred_element_type=F32)
    xr = jnp.dot(hm, w_in_ref[:, D_RNN:], preferred_element_type=F32)

    ext_scr[SUBLANES:SUBLANES + tm, :] = xr
    xc = conv_b_ref[...] + conv_w_ref[CONV_WIDTH - 1:CONV_WIDTH, :] * xr
    for k in range(CONV_WIDTH - 1):
        back = CONV_WIDTH - 1 - k
        xc = xc + conv_w_ref[k:k + 1, :] * ext_scr[SUBLANES - back:SUBLANES - back + tm, :]
    ext_scr[0:SUBLANES, :] = xr[tm - SUBLANES:, :]

    z = -a_param_ref[...]
    softplus = jnp.maximum(z, 0.0) + jnp.log(1.0 + jnp.exp(-jnp.abs(z)))
    decay = (-RG_LRU_C * math.log2(math.e)) * softplus
    xcb = xc.astype(BF16)
    for hd in range(RNN_HEADS):
        sl = slice(hd * RNN_BLOCK, (hd + 1) * RNN_BLOCK)
        gates = jnp.dot(xcb[:, sl], gate_w_ref[hd], preferred_element_type=F32) + gate_b_ref[hd]
        gates = jax.nn.sigmoid(gates)
        r = gates[:, :RNN_BLOCK]
        i = gates[:, RNN_BLOCK:]
        a = jnp.exp2(r * decay[:, sl])
        b = jnp.sqrt(1.0 - a * a) * (i * xc[:, sl])
        for jj in range(RNN_BLOCK // LANES):
            lt = hd * (RNN_BLOCK // LANES) + jj
            for sg in range(SUBLANES):
                rows = slice(sg * pitch, sg * pitch + seg)
                a_scr[lt, rows, :] = a[sg * seg:(sg + 1) * seg, jj * LANES:(jj + 1) * LANES]
                b_scr[lt, rows, :] = b[sg * seg:(sg + 1) * seg, jj * LANES:(jj + 1) * LANES]

    def scan_body(v, carry):
        h, ac = carry
        step = pl.ds(v, SUBLANES, stride=pitch)
        a = jnp.concatenate([a_scr[lt, step, :] for lt in range(n_lt)], axis=1)
        b = jnp.concatenate([b_scr[lt, step, :] for lt in range(n_lt)], axis=1)
        h = a * h + b
        ac = a * ac
        for lt in range(n_lt):
            h_scr[lt, step, :] = h[:, lt * LANES:(lt + 1) * LANES]
            ac_scr[lt, step, :] = ac[:, lt * LANES:(lt + 1) * LANES]
        return h, ac

    h_end, a_end = lax.fori_loop(
        0, seg, scan_body,
        (jnp.zeros((SUBLANES, D_RNN), F32), jnp.ones((SUBLANES, D_RNN), F32)))

    sub = lax.broadcasted_iota(jnp.int32, (SUBLANES, D_RNN), 0)
    prev = carry_scr[...]
    a, b = a_end, h_end
    for d in (1, 2, 4):
        keep = sub >= d
        a_sh = jnp.where(keep, pltpu.roll(a, d, axis=0), 1.0)
        b_sh = jnp.where(keep, pltpu.roll(b, d, axis=0), 0.0)
        b = a * b_sh + b
        a = a * a_sh
    seg_end = a * prev + b
    seg_in = jnp.where(sub >= 1, pltpu.roll(seg_end, 1, axis=0), prev)
    carry_scr[...] = jnp.broadcast_to(seg_end[SUBLANES - 1:SUBLANES, :], (SUBLANES, D_RNN))

    hs = jnp.concatenate(
        [jnp.concatenate(
            [h_scr[lt, sg * pitch:sg * pitch + seg, :]
             + ac_scr[lt, sg * pitch:sg * pitch + seg, :] * seg_in[sg:sg + 1, lt * LANES:(lt + 1) * LANES]
             for lt in range(n_lt)], axis=1)
         for sg in range(SUBLANES)], axis=0)
    mixed = (_gelu_tanh(y) * hs).astype(BF16)
    out = jnp.dot(mixed, w_out_ref[...], preferred_element_type=F32)
    x_new = x + _row(mod_ref, MIX_GATE) * out
    o_ref[...] = x_new
    _route_tile(x_new, gf_ref, mod_ref, w_hi_ref, w_lo_ref, b_ref, hfp_ref, ids_ref, wts_ref)


def _recurrent_layer(x, g, mod, w_in, conv_w, conv_b, gate_w, gate_b, a_param, w_out,
                     g_ffn, wr_hi, wr_lo, b_router, *, tm=512):
    nt = SEQ // tm
    const2 = lambda b, t: (0, 0)
    const3 = lambda b, t: (0, 0, 0)
    r_in, r_shape, r_out = _router_specs(tm, lambda b, t: b * nt + t)
    return pl.pallas_call(
        functools.partial(_rec_kernel, tm=tm),
        out_shape=(jax.ShapeDtypeStruct((TOKENS, D_MODEL), F32),) + r_shape,
        grid=(BATCH, nt),
        in_specs=[
            pl.BlockSpec((tm, D_MODEL), lambda b, t: (b * nt + t, 0)),
            pl.BlockSpec((1, D_MODEL), const2),
            pl.BlockSpec((None, 6, D_MODEL), lambda b, t: (b, 0, 0)),
            pl.BlockSpec((D_MODEL, 2 * D_RNN), const2),
            pl.BlockSpec((CONV_WIDTH, D_RNN), const2),
            pl.BlockSpec((1, D_RNN), const2),
            pl.BlockSpec((RNN_HEADS, RNN_BLOCK, 2 * RNN_BLOCK), const3),
            pl.BlockSpec((RNN_HEADS, 1, 2 * RNN_BLOCK), const3),
            pl.BlockSpec((1, D_RNN), const2),
            pl.BlockSpec((D_RNN, D_MODEL), const2),
        ] + r_in,
        out_specs=(pl.BlockSpec((tm, D_MODEL), lambda b, t: (b * nt + t, 0)),) + r_out,
        scratch_shapes=[
            pltpu.VMEM((tm + SUBLANES, D_RNN), F32),
            pltpu.VMEM((D_RNN // LANES, tm + SUBLANES * SUBLANES, LANES), F32),
            pltpu.VMEM((D_RNN // LANES, tm + SUBLANES * SUBLANES, LANES), F32),
            pltpu.VMEM((D_RNN // LANES, tm + SUBLANES * SUBLANES, LANES), F32),
            pltpu.VMEM((D_RNN // LANES, tm + SUBLANES * SUBLANES, LANES), F32),
            pltpu.VMEM((SUBLANES, D_RNN), F32),
        ],
        compiler_params=_params(("arbitrary", "arbitrary")),
        name="recurrent_layer",
    )(x, g.reshape(1, D_MODEL), mod, w_in, conv_w, conv_b.reshape(1, D_RNN), gate_w,
      gate_b.reshape(RNN_HEADS, 1, 2 * RNN_BLOCK), a_param.reshape(1, D_RNN), w_out,
      g_ffn.reshape(1, D_MODEL), wr_hi, wr_lo, b_router)


MOE_BLOCK = SEQ
PACK_ROWS = D_MODEL // (2 * LANES)
ROW_GRAN = 64
GRAN_SHIFT = ROW_GRAN.bit_length() - 1
TILE_STRIDE = ROW_GRAN + SUBLANES
TILE_ROWS = PACK_ROWS * TILE_STRIDE
N_TILES = TOP_K * MOE_BLOCK // ROW_GRAN + N_EXPERTS
SUPER_ROWS = 512
SUPER_SHIFT = SUPER_ROWS.bit_length() - 1
EXPERTS_PER_STEP = 2
N_EXPERT_STEPS = N_EXPERTS // EXPERTS_PER_STEP
EPI_ROWS = 512
N_EPI = MOE_BLOCK // EPI_ROWS
OUT_PITCH = EPI_ROWS + SUBLANES
ROUTER_TM = 512
GROUP_ROW0 = 0
EXPERT_ROW0 = SUBLANES
HI_MASK = 0xFFFF0000


def _pack_words(lo, hi):
    lo_bits = lax.bitcast_convert_type(lo.astype(BF16).astype(F32), jnp.uint32)
    hi_bits = lax.bitcast_convert_type(hi.astype(BF16).astype(F32), jnp.uint32)
    return (hi_bits & jnp.uint32(HI_MASK)) | (lo_bits >> jnp.uint32(16))


def _unpack_words(w):
    lo = lax.bitcast_convert_type(w << jnp.uint32(16), F32)
    hi = lax.bitcast_convert_type(w & jnp.uint32(HI_MASK), F32)
    return lo, hi


def _route_tile(x, g_ref, mod_ref, w_hi_ref, w_lo_ref, b_ref, hfp_ref, ids_ref, wts_ref):
    tm = x.shape[0]
    hf = _rms_mod(x, g_ref[...], _row(mod_ref, FFN_SHIFT), _row(mod_ref, FFN_SCALE))
    hi = hf.astype(BF16)
    hi32 = hi.astype(F32)
    half = D_MODEL // 2
    for j in range(PACK_ROWS):
        lo_bits = lax.bitcast_convert_type(hi32[:, j * LANES:(j + 1) * LANES], jnp.uint32)
        hi_bits = lax.bitcast_convert_type(hi32[:, half + j * LANES:half + (j + 1) * LANES],
                                           jnp.uint32)
        words = (hi_bits & jnp.uint32(HI_MASK)) | (lo_bits >> jnp.uint32(16))
        hfp_ref[pl.ds(j, tm, stride=PACK_ROWS), :] = words

    lo = (hf - hi32).astype(BF16)
    logits = (jnp.dot(hi, w_hi_ref[...], preferred_element_type=F32)
              + jnp.dot(lo, w_hi_ref[...], preferred_element_type=F32)
              + jnp.dot(hi, w_lo_ref[...], preferred_element_type=F32)) + b_ref[...]
    lt = logits.T

    neg = -jnp.inf
    row8 = lax.broadcasted_iota(jnp.int32, (SUBLANES, tm), 0)
    gl = jnp.where(row8 < N_GROUPS, lt[GROUP_ROW0:GROUP_ROW0 + SUBLANES, :], neg)
    gmax = jnp.max(gl, axis=0, keepdims=True)
    g_idx = jnp.min(jnp.where(gl == gmax, row8, SUBLANES), axis=0, keepdims=True)
    g_w = 1.0 / jnp.sum(jnp.exp(gl - gmax), axis=0, keepdims=True)

    el = lt[EXPERT_ROW0:EXPERT_ROW0 + N_EXPERTS, :]
    row16 = lax.broadcasted_iota(jnp.int32, (N_EXPERTS, tm), 0)
    emask = (row16 >> (EXPERTS_PER_GROUP.bit_length() - 1)) == g_idx
    v1 = jnp.max(jnp.where(emask, el, neg), axis=0, keepdims=True)
    i1 = jnp.min(jnp.where(emask & (el == v1), row16, N_EXPERTS), axis=0, keepdims=True)
    emask2 = emask & (row16 != i1)
    v2 = jnp.max(jnp.where(emask2, el, neg), axis=0, keepdims=True)
    i2 = jnp.min(jnp.where(emask2 & (el == v2), row16, N_EXPERTS), axis=0, keepdims=True)
    e2 = jnp.exp(v2 - v1)
    p1 = 1.0 / (1.0 + e2)
    p2 = e2 * p1
    ids_ref[...] = jnp.where(row8 == 0, i1, jnp.where(row8 == 1, i2, 0))
    wts_ref[...] = jnp.where(row8 == 0, p1 * g_w, jnp.where(row8 == 1, p2 * g_w, 0.0))


def _router_specs(tm, index):
    nt = TOKENS // tm
    assert tm == ROUTER_TM
    const = lambda *_: (0, 0)
    in_specs = [pl.BlockSpec((1, D_MODEL), const),
                pl.BlockSpec((D_MODEL, ROUTER_LANES), const),
                pl.BlockSpec((D_MODEL, ROUTER_LANES), const),
                pl.BlockSpec((1, ROUTER_LANES), const)]
    out_shape = (jax.ShapeDtypeStruct((TOKENS * PACK_ROWS, LANES), jnp.uint32),
                 jax.ShapeDtypeStruct((nt, SUBLANES, tm), jnp.int32),
                 jax.ShapeDtypeStruct((nt, SUBLANES, tm), F32))
    out_specs = (pl.BlockSpec((tm * PACK_ROWS, LANES), lambda *a: (index(*a), 0)),
                 pl.BlockSpec((None, SUBLANES, tm), lambda *a: (index(*a), 0, 0)),
                 pl.BlockSpec((None, SUBLANES, tm), lambda *a: (index(*a), 0, 0)))
    return in_specs, out_shape, out_specs


def _plan_kernel(ids_ref, wts_ref, addr_ref, gw_ref, meta_ref, rank_scr):
    nc, _, tm = ids_ref.shape
    row = lax.broadcasted_iota(jnp.int32, (N_EXPERTS, tm), 0)
    upper = (lax.broadcasted_iota(jnp.int32, (tm, tm), 0)
             < lax.broadcasted_iota(jnp.int32, (tm, tm), 1)).astype(BF16)

    def member(c):
        e0 = ids_ref[c, 0:1, :]
        e1 = ids_ref[c, 1:2, :]
        return e0, e1, ((row == e0) | (row == e1))

    counts = jnp.zeros((N_EXPERTS, 1), F32)
    for c in range(nc):
        _, _, mem = member(c)
        memf = mem.astype(F32)
        rank_scr[c] = jnp.dot(memf.astype(BF16), upper, preferred_element_type=F32) + counts
        counts = counts + jnp.sum(memf, axis=1, keepdims=True)

    tiles = jnp.floor((counts + (ROW_GRAN - 1)) * (1.0 / ROW_GRAN))
    lower = (lax.broadcasted_iota(jnp.int32, (N_EXPERTS, N_EXPERTS), 1)
             < lax.broadcasted_iota(jnp.int32, (N_EXPERTS, N_EXPERTS), 0)).astype(BF16)
    tiles_b = jnp.broadcast_to(tiles, (N_EXPERTS, LANES))
    row_start = ROW_GRAN * jnp.dot(lower, tiles_b.astype(BF16), preferred_element_type=F32)
    meta_ref[0] = row_start.astype(jnp.int32)
    meta_ref[1] = (ROW_GRAN * tiles_b).astype(jnp.int32)

    rs = row_start[:, 0:1]
    for c in range(nc):
        e0, e1, _ = member(c)
        where = rs + rank_scr[c]
        for k, ek in enumerate((e0, e1)):
            pos = jnp.sum(jnp.where(row == ek, where, 0.0), axis=0, keepdims=True).astype(jnp.int32)
            addr = (pos >> GRAN_SHIFT) * TILE_ROWS + (pos & (ROW_GRAN - 1))
            cols = slice(k * MOE_BLOCK + c * tm, k * MOE_BLOCK + (c + 1) * tm)
            addr_ref[:, cols] = addr
            gw_ref[:, cols] = wts_ref[c, k:k + 1, :]


def _plan(ids, wts):
    tm = ROUTER_TM
    nc = MOE_BLOCK // tm
    nb = TOKENS // MOE_BLOCK
    return pl.pallas_call(
        _plan_kernel,
        out_shape=(jax.ShapeDtypeStruct((nb, 1, TOP_K * MOE_BLOCK), jnp.int32),
                   jax.ShapeDtypeStruct((nb, 1, TOP_K * MOE_BLOCK), F32),
                   jax.ShapeDtypeStruct((nb, 2, N_EXPERTS, LANES), jnp.int32)),
        grid=(nb,),
        in_specs=[pl.BlockSpec((nc, SUBLANES, tm), lambda b: (b, 0, 0)),
                  pl.BlockSpec((nc, SUBLANES, tm), lambda b: (b, 0, 0))],
        out_specs=(pl.BlockSpec((None, 1, TOP_K * MOE_BLOCK), lambda b: (b, 0, 0)),
                   pl.BlockSpec((None, 1, TOP_K * MOE_BLOCK), lambda b: (b, 0, 0)),
                   pl.BlockSpec((None, 2, N_EXPERTS, LANES), lambda b: (b, 0, 0, 0))),
        scratch_shapes=[pltpu.VMEM((nc, N_EXPERTS, tm), F32)],
        compiler_params=_params(("arbitrary",)),
        name="moe_plan",
    )(ids, wts)


def _expert_rows(xy_ref, w13_ref, w2_ref, off, n_tiles):
    tile0 = lax.shift_right_logical(off, GRAN_SHIFT)
    bases = [pl.multiple_of((tile0 + ti) * TILE_ROWS, SUBLANES) for ti in range(n_tiles)]
    lo_cols, hi_cols = [], []
    for j in range(PACK_ROWS):
        words = jnp.concatenate(
            [xy_ref[pl.ds(b + j * TILE_STRIDE, ROW_GRAN), :] for b in bases], axis=0)
        lo, hi = _unpack_words(words)
        lo_cols.append(lo.astype(BF16))
        hi_cols.append(hi.astype(BF16))
    xin = jnp.concatenate(lo_cols + hi_cols, axis=1)
    u = jnp.dot(xin, w13_ref[...], preferred_element_type=F32)
    u1 = u[:, :D_EXPERT]
    act = (u1 * jax.nn.sigmoid(u1) * u[:, D_EXPERT:]).astype(BF16)
    y = jnp.dot(act, w2_ref[...], preferred_element_type=F32)
    half = D_MODEL // 2
    for j in range(PACK_ROWS):
        words = _pack_words(y[:, j * LANES:(j + 1) * LANES],
                            y[:, half + j * LANES:half + (j + 1) * LANES])
        for ti, b in enumerate(bases):
            xy_ref[pl.ds(b + j * TILE_STRIDE, ROW_GRAN), :] = words[ti * ROW_GRAN:(ti + 1) * ROW_GRAN]


def _moe_kernel(rs_ref, pad_ref, hfp_ref, addr_ref, gw_ref, w13_ref, w2_ref, x_ref, mod_ref,
                fg_ref, o_ref, xy_scr, out_scr, *, final_norm):
    b = pl.program_id(0)
    s = pl.program_id(1)
    unroll = SUBLANES

    @pl.when((b == 0) & (s == 0))
    def _():
        xy_scr[...] = jnp.zeros(xy_scr.shape, jnp.uint32)

    @pl.when(s == 0)
    def _():
        def group(g, carry):
            for i in range(unroll):
                t = g * unroll + i
                slab = hfp_ref[pl.ds(pl.multiple_of(g * (unroll * PACK_ROWS), unroll * PACK_ROWS)
                                     + i * PACK_ROWS, PACK_ROWS), :]
                for k in range(TOP_K):
                    xy_scr[pl.ds(addr_ref[0, k * MOE_BLOCK + t], PACK_ROWS, stride=TILE_STRIDE), :] = slab
            return carry
        lax.fori_loop(0, MOE_BLOCK // unroll, group, 0)

    def run_expert(slot):
        e = b * N_EXPERTS + s * EXPERTS_PER_STEP + slot
        row0 = rs_ref[e]
        padded = pad_ref[e]

        def super_tile(i, carry):
            off = row0 + i * SUPER_ROWS
            n_tiles = lax.shift_right_logical(
                jnp.minimum(padded - i * SUPER_ROWS, SUPER_ROWS), GRAN_SHIFT)
            for m in range(1, SUPER_ROWS // ROW_GRAN + 1):
                @pl.when(n_tiles == m)
                def _(m=m):
                    _expert_rows(xy_scr, w13_ref.at[slot], w2_ref.at[slot], off, m)
            return carry
        lax.fori_loop(0, lax.shift_right_logical(padded + (SUPER_ROWS - 1), SUPER_SHIFT), super_tile, 0)

    @pl.when(s < N_EXPERT_STEPS)
    def _():
        for slot in range(EXPERTS_PER_STEP):
            run_expert(slot)

    @pl.when(s >= N_EXPERT_STEPS)
    def _():
        t0 = (s - N_EXPERT_STEPS) * EPI_ROWS

        def group(g, carry):
            for i in range(unroll):
                tl = g * unroll + i
                lo_acc = hi_acc = None
                for k in range(TOP_K):
                    pair = k * MOE_BLOCK + t0 + tl
                    words = xy_scr[pl.ds(addr_ref[0, pair], PACK_ROWS, stride=TILE_STRIDE), :]
                    lo, hi = _unpack_words(words)
                    wk = gw_ref[0, pair]
                    lo_acc = wk * lo if lo_acc is None else lo_acc + wk * lo
                    hi_acc = wk * hi if hi_acc is None else hi_acc + wk * hi
                out_scr[pl.ds(tl, PACK_ROWS, stride=OUT_PITCH), :] = lo_acc
                out_scr[pl.ds(PACK_ROWS * OUT_PITCH + tl, PACK_ROWS, stride=OUT_PITCH), :] = hi_acc
            return carry
        lax.fori_loop(0, EPI_ROWS // unroll, group, 0)

        for j in range(D_MODEL // LANES):
            cols = slice(j * LANES, (j + 1) * LANES)
            moe = out_scr[j * OUT_PITCH:j * OUT_PITCH + EPI_ROWS, :]
            o_ref[:, cols] = x_ref[:, cols] + _row(mod_ref, FFN_GATE, cols) * moe
        if final_norm:
            v = o_ref[...]
            o_ref[...] = v * lax.rsqrt(jnp.mean(v * v, axis=-1, keepdims=True) + NORM_EPS) * fg_ref[...]


def _moe_sparse(hfp, addr, gw, row_start, padded, w13, w2, x, mod, final_g, *, layer, final_norm):
    nb = TOKENS // MOE_BLOCK
    steps = N_EXPERT_STEPS + N_EPI

    def epi_block(b, s, *_):
        return (b * N_EPI + jnp.maximum(s - N_EXPERT_STEPS, 0), 0)

    def expert_block(b, s, *_):
        return (layer, jnp.minimum(s, N_EXPERT_STEPS - 1), 0, 0)

    grid_spec = pltpu.PrefetchScalarGridSpec(
        num_scalar_prefetch=2,
        grid=(nb, steps),
        in_specs=[
            pl.BlockSpec((MOE_BLOCK * PACK_ROWS, LANES), lambda b, s, *_: (b, 0)),
            pl.BlockSpec((None, 1, TOP_K * MOE_BLOCK), lambda b, s, *_: (b, 0, 0),
                         memory_space=pltpu.SMEM),
            pl.BlockSpec((None, 1, TOP_K * MOE_BLOCK), lambda b, s, *_: (b, 0, 0),
                         memory_space=pltpu.SMEM),
            pl.BlockSpec((None, EXPERTS_PER_STEP, D_MODEL, 2 * D_EXPERT), expert_block),
            pl.BlockSpec((None, EXPERTS_PER_STEP, D_EXPERT, D_MODEL), expert_block),
            pl.BlockSpec((EPI_ROWS, D_MODEL), epi_block),
            pl.BlockSpec((None, 6, D_MODEL), lambda b, s, *_: (b, 0, 0)),
            pl.BlockSpec((1, D_MODEL), lambda b, s, *_: (0, 0)),
        ],
        out_specs=pl.BlockSpec((EPI_ROWS, D_MODEL), epi_block),
        scratch_shapes=[
            pltpu.VMEM((N_TILES * TILE_ROWS, LANES), jnp.uint32),
            pltpu.VMEM((D_MODEL // LANES * OUT_PITCH, LANES), F32),
        ],
    )
    return pl.pallas_call(
        functools.partial(_moe_kernel, final_norm=final_norm),
        out_shape=jax.ShapeDtypeStruct((TOKENS, D_MODEL), F32),
        grid_spec=grid_spec,
        compiler_params=_params(("arbitrary", "arbitrary")),
        name="moe_experts",
    )(row_start, padded, hfp, addr, gw, w13, w2, x, mod,
      final_g.reshape(1, D_MODEL))


def _lambda_init(layer):
    return 0.8 - 0.6 * math.exp(-0.3 * layer)


def _split_bf16(w):
    hi = w.astype(BF16)
    return hi, (w - hi.astype(F32)).astype(BF16)


def kernel(x, c, norm_mix, norm_ffn, final_norm, ada_w, ada_b, attn_w_qkv, attn_lambda, attn_subln, attn_w_o, rec_w_in, rec_conv_w, rec_conv_b, rec_gate_w, rec_gate_b, rec_a_param, rec_w_out, moe_w_group, moe_b_group, moe_w_expert, moe_b_expert, moe_w13, moe_w2):
    xt = x.reshape(TOKENS, D_MODEL)
    mod_all = _ada_mod(c, ada_w, ada_b).reshape(DEPTH, BATCH, 6, D_MODEL)

    def lane_pad(n, *lead):
        return jnp.zeros(lead + (n,), F32)

    gap = EXPERT_ROW0 - N_GROUPS
    tail = ROUTER_LANES - EXPERT_ROW0 - N_EXPERTS
    w_router = jnp.concatenate([moe_w_group, lane_pad(gap, DEPTH, D_MODEL), moe_w_expert,
                                lane_pad(tail, DEPTH, D_MODEL)], axis=-1)
    b_router = jnp.concatenate([moe_b_group, lane_pad(gap, DEPTH), moe_b_expert,
                                lane_pad(tail, DEPTH)], axis=-1)
    wr_hi, wr_lo = _split_bf16(w_router)

    w_qkv = attn_w_qkv.astype(BF16)
    w_o = attn_w_o.astype(BF16)
    w_in = rec_w_in.astype(BF16)
    w_gate = rec_gate_w.astype(BF16)
    w_out = rec_w_out.astype(BF16)
    w13 = moe_w13.astype(BF16)
    w2 = moe_w2.astype(BF16)

    for layer in range(DEPTH):
        mod = mod_all[layer]
        j = layer // N_MIXERS
        route = (norm_ffn[layer], wr_hi[layer], wr_lo[layer], b_router[layer].reshape(1, ROUTER_LANES))
        if layer % N_MIXERS == 0:
            qkv = _norm_proj(xt, norm_mix[layer], mod, w_qkv[j], out_scales=(Q_PRESCALE, 1.0, 1.0))
            o = _diff_attention(qkv, attn_lambda[j], attn_subln[j], _lambda_init(layer))
            xt, hfp, ids, wts = _proj_residual(o, w_o[j], xt, mod, *route)
        else:
            xt, hfp, ids, wts = _recurrent_layer(
                xt, norm_mix[layer], mod, w_in[j], rec_conv_w[j], rec_conv_b[j], w_gate[j],
                rec_gate_b[j], rec_a_param[j], w_out[j], *route)
        addr, gw, meta = _plan(ids, wts)
        xt = _moe_sparse(hfp, addr, gw, meta[:, 0, :, 0].reshape(-1), meta[:, 1, :, 0].reshape(-1),
                         w13, w2, xt, mod, final_norm,
                         layer=layer, final_norm=(layer == DEPTH - 1))

    return xt.reshape(BATCH, SEQ, D_MODEL)
```

```python
import functools
import math

import jax
import jax.numpy as jnp
from jax import lax
from jax.experimental import pallas as pl
from jax.experimental.pallas import tpu as pltpu

D_MODEL = 1024
BATCH = 32
SEQ = 2048
TOKENS = BATCH * SEQ
DEPTH = 4
N_MIXERS = 2
HEAD_DIM = 64
ATTN_HEADS = D_MODEL // (2 * HEAD_DIM)
ATTN_WIDTH = ATTN_HEADS * 2 * HEAD_DIM
SUBLN_EPS = 1e-5
D_RNN = D_MODEL
RNN_HEADS = 4
RNN_BLOCK = D_RNN // RNN_HEADS
CONV_WIDTH = 4
RG_LRU_C = 8.0
N_GROUPS = 4
EXPERTS_PER_GROUP = 4
N_EXPERTS = N_GROUPS * EXPERTS_PER_GROUP
TOP_K = 2
D_EXPERT = 512
NORM_EPS = 1e-6
Q_PRESCALE = HEAD_DIM ** -0.5 * math.log2(math.e)

LANES = 128
SUBLANES = 8
ROUTER_LANES = LANES
SCAN_PITCH_PAD = 4
V7X_VMEM_BYTES = 64 * 1024 * 1024
VMEM_LIMIT = V7X_VMEM_BYTES * 7 // 8
MIX_SHIFT, MIX_SCALE, MIX_GATE, FFN_SHIFT, FFN_SCALE, FFN_GATE = range(6)

F32 = jnp.float32
BF16 = jnp.bfloat16


def _params(semantics):
    return pltpu.CompilerParams(dimension_semantics=semantics, vmem_limit_bytes=VMEM_LIMIT)


def _row(ref, r, cols=slice(None)):
    return ref[r:r + 1, cols]


def _rms_mod(x, g, shift, scale, eps=NORM_EPS):
    y = x * lax.rsqrt(jnp.mean(x * x, axis=-1, keepdims=True) + eps) * g
    return y * (1.0 + scale) + shift


def _ada_kernel(c_ref, w_ref, b_ref, o_ref):
    c = c_ref[...]
    cond = (c * jax.nn.sigmoid(c)).astype(BF16)
    o_ref[...] = jnp.dot(cond, w_ref[...].astype(BF16), preferred_element_type=F32) + b_ref[...]


def _ada_mod(c, ada_w, ada_b):
    tn = 1536
    n = 6 * D_MODEL
    return pl.pallas_call(
        _ada_kernel,
        out_shape=jax.ShapeDtypeStruct((DEPTH, BATCH, n), F32),
        grid=(DEPTH, n // tn),
        in_specs=[
            pl.BlockSpec((BATCH, D_MODEL), lambda l, j: (0, 0)),
            pl.BlockSpec((None, D_MODEL, tn), lambda l, j: (l, 0, j)),
            pl.BlockSpec((None, 1, tn), lambda l, j: (l, 0, j)),
        ],
        out_specs=pl.BlockSpec((None, BATCH, tn), lambda l, j: (l, 0, j)),
        compiler_params=_params(("arbitrary", "arbitrary")),
        name="ada_mod",
    )(c, ada_w, ada_b.reshape(DEPTH, 1, n))


def _norm_proj_kernel(x_ref, g_ref, mod_ref, w_ref, o_ref, *, tn, out_scales):
    h = _rms_mod(x_ref[...], g_ref[...], _row(mod_ref, MIX_SHIFT), _row(mod_ref, MIX_SCALE)).astype(BF16)
    for j in range(o_ref.shape[1] // tn):
        y = jnp.dot(h, w_ref[:, j * tn:(j + 1) * tn], preferred_element_type=F32)
        if out_scales[j] != 1.0:
            y = y * out_scales[j]
        o_ref[:, j * tn:(j + 1) * tn] = y.astype(o_ref.dtype)


def _norm_proj(x, g, mod, w, *, out_scales, tm=512, tn=1024):
    n = w.shape[1]
    assert len(out_scales) == n // tn
    return pl.pallas_call(
        functools.partial(_norm_proj_kernel, tn=tn, out_scales=out_scales),
        out_shape=jax.ShapeDtypeStruct((TOKENS, n), BF16),
        grid=(TOKENS // tm,),
        in_specs=[
            pl.BlockSpec((tm, D_MODEL), lambda i: (i, 0)),
            pl.BlockSpec((1, D_MODEL), lambda i: (0, 0)),
            pl.BlockSpec((None, 6, D_MODEL), lambda i: (i // (SEQ // tm), 0, 0)),
            pl.BlockSpec((D_MODEL, n), lambda i: (0, 0)),
        ],
        out_specs=pl.BlockSpec((tm, n), lambda i: (i, 0)),
        compiler_params=_params(("arbitrary",)),
        name="norm_proj",
    )(x, g.reshape(1, D_MODEL), mod, w)


def _attn_kernel(lam_ref, sg_ref, q_ref, k_ref, v_ref, o_ref, qm_scr, ve_scr, m_scr, acc_scr,
                 *, tq, lambda_init):
    qi = pl.program_id(1)
    hw = 2 * HEAD_DIM
    lv = lam_ref[...]
    lam = (jnp.exp(jnp.sum(lv[0:1] * lv[1:2], keepdims=True))
           - jnp.exp(jnp.sum(lv[2:3] * lv[3:4], keepdims=True)) + lambda_init)

    lane = lax.broadcasted_iota(jnp.int32, (tq, hw), 1)
    for h in range(ATTN_HEADS):
        q = q_ref[:, h * hw:(h + 1) * hw]
        qm_scr[2 * h] = jnp.where(lane < HEAD_DIM, q, jnp.zeros_like(q))
        qm_scr[2 * h + 1] = jnp.where(lane >= HEAD_DIM, q, jnp.zeros_like(q))

    @pl.when(qi == 0)
    def _():
        ones = jnp.ones((SEQ, hw), BF16)
        for h in range(ATTN_HEADS):
            ve_scr[:, 2 * h * hw:(2 * h + 1) * hw] = v_ref[:, h * hw:(h + 1) * hw]
            ve_scr[:, (2 * h + 1) * hw:(2 * h + 2) * hw] = ones

    def step(j, masked, first):
        start = pl.multiple_of(j * tq, tq)
        if masked:
            row = lax.broadcasted_iota(jnp.int32, (tq, tq), 0)
            col = lax.broadcasted_iota(jnp.int32, (tq, tq), 1)
            keep = row >= col
        for h in range(ATTN_HEADS):
            kb = k_ref[pl.ds(start, tq), h * hw:(h + 1) * hw]
            vb = ve_scr[pl.ds(start, tq), 2 * h * hw:(2 * h + 2) * hw]
            for mi in range(2):
                c = 2 * h + mi
                s = lax.dot_general(qm_scr[c], kb, (((1,), (1,)), ((), ())),
                                    preferred_element_type=F32)
                if masked:
                    s = jnp.where(keep, s, -jnp.inf)
                m_cur = jnp.max(s, axis=-1, keepdims=True)
                if first:
                    m_new = jnp.broadcast_to(m_cur, (tq, LANES))
                else:
                    m_prev = m_scr[c]
                    m_new = jnp.maximum(m_prev, m_cur)
                    alpha = jnp.exp2(m_prev - m_new)
                p = jnp.exp2(s - jnp.concatenate([m_new] * (tq // LANES), axis=1))
                pv = jnp.dot(p.astype(BF16), vb, preferred_element_type=F32)
                if first:
                    acc_scr[c] = pv
                else:
                    acc_scr[c] = jnp.concatenate([alpha, alpha], axis=1) * acc_scr[c] + pv
                m_scr[c] = m_new

    step(qi, True, True)

    def body(jj, carry):
        step(2 * jj, False, False)
        step(2 * jj + 1, False, False)
        return carry

    lax.fori_loop(0, qi // 2, body, 0)

    @pl.when(qi % 2 == 1)
    def _():
        step(qi - 1, False, False)

    for h in range(ATTN_HEADS):
        a1 = acc_scr[2 * h]
        a2 = acc_scr[2 * h + 1]
        o = a1[:, :hw] / a1[:, hw:] - lam * (a2[:, :hw] / a2[:, hw:])
        o = o * lax.rsqrt(jnp.mean(o * o, axis=-1, keepdims=True) + SUBLN_EPS) * sg_ref[...]
        o_ref[:, h * hw:(h + 1) * hw] = (o * (1.0 - lambda_init)).astype(o_ref.dtype)


def _diff_attention(qkv, lam_vec, subln_g, lambda_init, *, tq=256):
    nq = SEQ // tq
    hw = 2 * HEAD_DIM
    assert hw == LANES
    chains = 2 * ATTN_HEADS
    return pl.pallas_call(
        functools.partial(_attn_kernel, tq=tq, lambda_init=lambda_init),
        out_shape=jax.ShapeDtypeStruct((TOKENS, ATTN_WIDTH), BF16),
        grid=(BATCH, nq),
        in_specs=[
            pl.BlockSpec((4, HEAD_DIM), lambda b, i: (0, 0)),
            pl.BlockSpec((1, hw), lambda b, i: (0, 0)),
            pl.BlockSpec((tq, ATTN_WIDTH), lambda b, i: (b * nq + i, 0)),
            pl.BlockSpec((SEQ, ATTN_WIDTH), lambda b, i: (b, 1)),
            pl.BlockSpec((SEQ, ATTN_WIDTH), lambda b, i: (b, 2)),
        ],
        out_specs=pl.BlockSpec((tq, ATTN_WIDTH), lambda b, i: (b * nq + i, 0)),
        scratch_shapes=[
            pltpu.VMEM((chains, tq, hw), BF16),
            pltpu.VMEM((SEQ, 2 * ATTN_WIDTH), BF16),
            pltpu.VMEM((chains, tq, LANES), F32),
            pltpu.VMEM((chains, tq, 2 * hw), F32),
        ],
        compiler_params=_params(("arbitrary", "arbitrary")),
        name="diff_attention",
    )(lam_vec, subln_g.reshape(1, hw), qkv, qkv, qkv)


def _proj_residual_kernel(a_ref, w_ref, x_ref, mod_ref, gf_ref, w_hi_ref, w_lo_ref, b_ref,
                          o_ref, hfp_ref, ids_ref, wts_ref):
    y = jnp.dot(a_ref[...], w_ref[...], preferred_element_type=F32)
    x_new = x_ref[...] + _row(mod_ref, MIX_GATE) * y
    o_ref[...] = x_new
    _route_tile(x_new, gf_ref, mod_ref, w_hi_ref, w_lo_ref, b_ref, hfp_ref, ids_ref, wts_ref)


def _proj_residual(a, w, x, mod, g_ffn, wr_hi, wr_lo, b_router, *, tm=512):
    k = a.shape[1]
    r_in, r_shape, r_out = _router_specs(tm, lambda i: i)
    return pl.pallas_call(
        _proj_residual_kernel,
        out_shape=(jax.ShapeDtypeStruct((TOKENS, D_MODEL), F32),) + r_shape,
        grid=(TOKENS // tm,),
        in_specs=[
            pl.BlockSpec((tm, k), lambda i: (i, 0)),
            pl.BlockSpec((k, D_MODEL), lambda i: (0, 0)),
            pl.BlockSpec((tm, D_MODEL), lambda i: (i, 0)),
            pl.BlockSpec((None, 6, D_MODEL), lambda i: (i // (SEQ // tm), 0, 0)),
        ] + r_in,
        out_specs=(pl.BlockSpec((tm, D_MODEL), lambda i: (i, 0)),) + r_out,
        compiler_params=_params(("arbitrary",)),
        name="proj_residual",
    )(a, w, x, mod, g_ffn.reshape(1, D_MODEL), wr_hi, wr_lo, b_router)


def _gelu_tanh(y):
    c = math.sqrt(2.0 / math.pi)
    return 0.5 * y * (1.0 + jnp.tanh(c * (y + 0.044715 * (y * y * y))))


def _rec_kernel(x_ref, g_ref, mod_ref, w_in_ref, conv_w_ref, conv_b_ref, gate_w_ref, gate_b_ref,
                a_param_ref, w_out_ref, gf_ref, w_hi_ref, w_lo_ref, b_ref,
                o_ref, hfp_ref, ids_ref, wts_ref,
                ext_scr, a_scr, b_scr, h_scr, ac_scr, carry_scr, *, tm):
    t = pl.program_id(1)
    seg = tm // SUBLANES
    pitch = seg + SCAN_PITCH_PAD
    n_lt = D_RNN // LANES

    @pl.when(t == 0)
    def _():
        carry_scr[...] = jnp.zeros(carry_scr.shape, F32)
        ext_scr[0:SUBLANES, :] = jnp.zeros((SUBLANES, D_RNN), F32)

    x = x_ref[...]
    hm = _rms_mod(x, g_ref[...], _row(mod_ref, MIX_SHIFT), _row(mod_ref, MIX_SCALE)).astype(BF16)
    y = jnp.dot(hm, w_in_ref[:, :D_RNN], preferred_element_type=F32)
    xr = jnp.dot(hm, w_in_ref[:, D_RNN:], preferred_element_type=F32)

    ext_scr[SUBLANES:SUBLANES + tm, :] = xr
    xc = conv_b_ref[...] + conv_w_ref[CONV_WIDTH - 1:CONV_WIDTH, :] * xr
    for k in range(CONV_WIDTH - 1):
        back = CONV_WIDTH - 1 - k
        xc = xc + conv_w_ref[k:k + 1, :] * ext_scr[SUBLANES - back:SUBLANES - back + tm, :]
    ext_scr[0:SUBLANES, :] = xr[tm - SUBLANES:, :]

    z = -a_param_ref[...]
    softplus = jnp.maximum(z, 0.0) + jnp.log(1.0 + jnp.exp(-jnp.abs(z)))
    decay = (-RG_LRU_C * math.log2(math.e)) * softplus
    xcb = xc.astype(BF16)
    for hd in range(RNN_HEADS):
        sl = slice(hd * RNN_BLOCK, (hd + 1) * RNN_BLOCK)
        gates = jnp.dot(xcb[:, sl], gate_w_ref[hd], preferred_element_type=F32) + gate_b_ref[hd]
        gates = jax.nn.sigmoid(gates)
        r = gates[:, :RNN_BLOCK]
        i = gates[:, RNN_BLOCK:]
        a = jnp.exp2(r * decay[:, sl])
        b = jnp.sqrt(1.0 - a * a) * (i * xc[:, sl])
        for jj in range(RNN_BLOCK // LANES):
            lt = hd * (RNN_BLOCK // LANES) + jj
            for sg in range(SUBLANES):
                rows = slice(sg * pitch, sg * pitch + seg)
                a_scr[lt, rows, :] = a[sg * seg:(sg + 1) * seg, jj * LANES:(jj + 1) * LANES]
                b_scr[lt, rows, :] = b[sg * seg:(sg + 1) * seg, jj * LANES:(jj + 1) * LANES]

    def scan_body(v, carry):
        h, ac = carry
        step = pl.ds(v, SUBLANES, stride=pitch)
        a = jnp.concatenate([a_scr[lt, step, :] for lt in range(n_lt)], axis=1)
        b = jnp.concatenate([b_scr[lt, step, :] for lt in range(n_lt)], axis=1)
        h = a * h + b
        ac = a * ac
        for lt in range(n_lt):
            h_scr[lt, step, :] = h[:, lt * LANES:(lt + 1) * LANES]
            ac_scr[lt, step, :] = ac[:, lt * LANES:(lt + 1) * LANES]
        return h, ac

    h_end, a_end = lax.fori_loop(
        0, seg, scan_body,
        (jnp.zeros((SUBLANES, D_RNN), F32), jnp.ones((SUBLANES, D_RNN), F32)))

    sub = lax.broadcasted_iota(jnp.int32, (SUBLANES, D_RNN), 0)
    prev = carry_scr[...]
    a, b = a_end, h_end
    for d in (1, 2, 4):
        keep = sub >= d
        a_sh = jnp.where(keep, pltpu.roll(a, d, axis=0), 1.0)
        b_sh = jnp.where(keep, pltpu.roll(b, d, axis=0), 0.0)
        b = a * b_sh + b
        a = a * a_sh
    seg_end = a * prev + b
    seg_in = jnp.where(sub >= 1, pltpu.roll(seg_end, 1, axis=0), prev)
    carry_scr[...] = jnp.broadcast_to(seg_end[SUBLANES - 1:SUBLANES, :], (SUBLANES, D_RNN))

    hs = jnp.concatenate(
        [jnp.concatenate(
            [h_scr[lt, sg * pitch:sg * pitch + seg, :]
             + ac_scr[lt, sg * pitch:sg * pitch + seg, :] * seg_in[sg:sg + 1, lt * LANES:(lt + 1) * LANES]
             for lt in range(n_lt)], axis=1)
         for sg in range(SUBLANES)], axis=0)
    mixed = (_gelu_tanh(y) * hs).astype(BF16)
    out = jnp.dot(mixed, w_out_ref[...], preferred_element_type=F32)
    x_new = x + _row(mod_ref, MIX_GATE) * out
    o_ref[...] = x_new
    _route_tile(x_new, gf_ref, mod_ref, w_hi_ref, w_lo_ref, b_ref, hfp_ref, ids_ref, wts_ref)


def _recurrent_layer(x, g, mod, w_in, conv_w, conv_b, gate_w, gate_b, a_param, w_out,
                     g_ffn, wr_hi, wr_lo, b_router, *, tm=512):
    nt = SEQ // tm
    const2 = lambda b, t: (0, 0)
    const3 = lambda b, t: (0, 0, 0)
    r_in, r_shape, r_out = _router_specs(tm, lambda b, t: b * nt + t)
    return pl.pallas_call(
        functools.partial(_rec_kernel, tm=tm),
        out_shape=(jax.ShapeDtypeStruct((TOKENS, D_MODEL), F32),) + r_shape,
        grid=(BATCH, nt),
        in_specs=[
            pl.BlockSpec((tm, D_MODEL), lambda b, t: (b * nt + t, 0)),
            pl.BlockSpec((1, D_MODEL), const2),
            pl.BlockSpec((None, 6, D_MODEL), lambda b, t: (b, 0, 0)),
            pl.BlockSpec((D_MODEL, 2 * D_RNN), const2),
            pl.BlockSpec((CONV_WIDTH, D_RNN), const2),
            pl.BlockSpec((1, D_RNN), const2),
            pl.BlockSpec((RNN_HEADS, RNN_BLOCK, 2 * RNN_BLOCK), const3),
            pl.BlockSpec((RNN_HEADS, 1, 2 * RNN_BLOCK), const3),
            pl.BlockSpec((1, D_RNN), const2),
            pl.BlockSpec((D_RNN, D_MODEL), const2),
        ] + r_in,
        out_specs=(pl.BlockSpec((tm, D_MODEL), lambda b, t: (b * nt + t, 0)),) + r_out,
        scratch_shapes=[
            pltpu.VMEM((tm + SUBLANES, D_RNN), F32),
            pltpu.VMEM((D_RNN // LANES, tm + SUBLANES * SUBLANES, LANES), F32),
            pltpu.VMEM((D_RNN // LANES, tm + SUBLANES * SUBLANES, LANES), F32),
            pltpu.VMEM((D_RNN // LANES, tm + SUBLANES * SUBLANES, LANES), F32),
            pltpu.VMEM((D_RNN // LANES, tm + SUBLANES * SUBLANES, LANES), F32),
            pltpu.VMEM((SUBLANES, D_RNN), F32),
        ],
        compiler_params=_params(("arbitrary", "arbitrary")),
        name="recurrent_layer",
    )(x, g.reshape(1, D_MODEL), mod, w_in, conv_w, conv_b.reshape(1, D_RNN), gate_w,
      gate_b.reshape(RNN_HEADS, 1, 2 * RNN_BLOCK), a_param.reshape(1, D_RNN), w_out,
      g_ffn.reshape(1, D_MODEL), wr_hi, wr_lo, b_router)


MOE_BLOCK = SEQ
PACK_ROWS = D_MODEL // (2 * LANES)
ROW_GRAN = 64
GRAN_SHIFT = ROW_GRAN.bit_length() - 1
TILE_STRIDE = ROW_GRAN + SUBLANES
TILE_ROWS = PACK_ROWS * TILE_STRIDE
N_TILES = TOP_K * MOE_BLOCK // ROW_GRAN + N_EXPERTS
SUPER_ROWS = 512
SUPER_SHIFT = SUPER_ROWS.bit_length() - 1
EXPERTS_PER_STEP = 2
N_EXPERT_STEPS = N_EXPERTS // EXPERTS_PER_STEP
EPI_ROWS = 512
N_EPI = MOE_BLOCK // EPI_ROWS
OUT_PITCH = EPI_ROWS + SUBLANES
ROUTER_TM = 512
GROUP_ROW0 = 0
EXPERT_ROW0 = SUBLANES
HI_MASK = 0xFFFF0000


def _pack_words(lo, hi):
    lo_bits = lax.bitcast_convert_type(lo.astype(BF16).astype(F32), jnp.uint32)
    hi_bits = lax.bitcast_convert_type(hi.astype(BF16).astype(F32), jnp.uint32)
    return (hi_bits & jnp.uint32(HI_MASK)) | (lo_bits >> jnp.uint32(16))


def _unpack_words(w):
    lo = lax.bitcast_convert_type(w << jnp.uint32(16), F32)
    hi = lax.bitcast_convert_type(w & jnp.uint32(HI_MASK), F32)
    return lo, hi


def _route_tile(x, g_ref, mod_ref, w_hi_ref, w_lo_ref, b_ref, hfp_ref, ids_ref, wts_ref):
    tm = x.shape[0]
    hf = _rms_mod(x, g_ref[...], _row(mod_ref, FFN_SHIFT), _row(mod_ref, FFN_SCALE))
    hi = hf.astype(BF16)
    hi32 = hi.astype(F32)
    half = D_MODEL // 2
    for j in range(PACK_ROWS):
        lo_bits = lax.bitcast_convert_type(hi32[:, j * LANES:(j + 1) * LANES], jnp.uint32)
        hi_bits = lax.bitcast_convert_type(hi32[:, half + j * LANES:half + (j + 1) * LANES],
                                           jnp.uint32)
        words = (hi_bits & jnp.uint32(HI_MASK)) | (lo_bits >> jnp.uint32(16))
        hfp_ref[pl.ds(j, tm, stride=PACK_ROWS), :] = words

    lo = (hf - hi32).astype(BF16)
    logits = (jnp.dot(hi, w_hi_ref[...], preferred_element_type=F32)
              + jnp.dot(lo, w_hi_ref[...], preferred_element_type=F32)
              + jnp.dot(hi, w_lo_ref[...], preferred_element_type=F32)) + b_ref[...]
    lt = logits.T

    neg = -jnp.inf
    row8 = lax.broadcasted_iota(jnp.int32, (SUBLANES, tm), 0)
    gl = jnp.where(row8 < N_GROUPS, lt[GROUP_ROW0:GROUP_ROW0 + SUBLANES, :], neg)
    gmax = jnp.max(gl, axis=0, keepdims=True)
    g_idx = jnp.min(jnp.where(gl == gmax, row8, SUBLANES), axis=0, keepdims=True)
    g_w = 1.0 / jnp.sum(jnp.exp(gl - gmax), axis=0, keepdims=True)

    el = lt[EXPERT_ROW0:EXPERT_ROW0 + N_EXPERTS, :]
    row16 = lax.broadcasted_iota(jnp.int32, (N_EXPERTS, tm), 0)
    emask = (row16 >> (EXPERTS_PER_GROUP.bit_length() - 1)) == g_idx
    v1 = jnp.max(jnp.where(emask, el, neg), axis=0, keepdims=True)
    i1 = jnp.min(jnp.where(emask & (el == v1), row16, N_EXPERTS), axis=0, keepdims=True)
    emask2 = emask & (row16 != i1)
    v2 = jnp.max(jnp.where(emask2, el, neg), axis=0, keepdims=True)
    i2 = jnp.min(jnp.where(emask2 & (el == v2), row16, N_EXPERTS), axis=0, keepdims=True)
    e2 = jnp.exp(v2 - v1)
    p1 = 1.0 / (1.0 + e2)
    p2 = e2 * p1
    ids_ref[...] = jnp.where(row8 == 0, i1, jnp.where(row8 == 1, i2, 0))
    wts_ref[...] = jnp.where(row8 == 0, p1 * g_w, jnp.where(row8 == 1, p2 * g_w, 0.0))


def _router_specs(tm, index):
    nt = TOKENS // tm
    assert tm == ROUTER_TM
    const = lambda *_: (0, 0)
    in_specs = [pl.BlockSpec((1, D_MODEL), const),
                pl.BlockSpec((D_MODEL, ROUTER_LANES), const),
                pl.BlockSpec((D_MODEL, ROUTER_LANES), const),
                pl.BlockSpec((1, ROUTER_LANES), const)]
    out_shape = (jax.ShapeDtypeStruct((TOKENS * PACK_ROWS, LANES), jnp.uint32),
                 jax.ShapeDtypeStruct((nt, SUBLANES, tm), jnp.int32),
                 jax.ShapeDtypeStruct((nt, SUBLANES, tm), F32))
    out_specs = (pl.BlockSpec((tm * PACK_ROWS, LANES), lambda *a: (index(*a), 0)),
                 pl.BlockSpec((None, SUBLANES, tm), lambda *a: (index(*a), 0, 0)),
                 pl.BlockSpec((None, SUBLANES, tm), lambda *a: (index(*a), 0, 0)))
    return in_specs, out_shape, out_specs


def _plan_kernel(ids_ref, wts_ref, addr_ref, gw_ref, meta_ref, rank_scr):
    nc, _, tm = ids_ref.shape
    row = lax.broadcasted_iota(jnp.int32, (N_EXPERTS, tm), 0)
    upper = (lax.broadcasted_iota(jnp.int32, (tm, tm), 0)
             < lax.broadcasted_iota(jnp.int32, (tm, tm), 1)).astype(BF16)

    def member(c):
        e0 = ids_ref[c, 0:1, :]
        e1 = ids_ref[c, 1:2, :]
        return e0, e1, ((row == e0) | (row == e1))

    counts = jnp.zeros((N_EXPERTS, 1), F32)
    for c in range(nc):
        _, _, mem = member(c)
        memf = mem.astype(F32)
        rank_scr[c] = jnp.dot(memf.astype(BF16), upper, preferred_element_type=F32) + counts
        counts = counts + jnp.sum(memf, axis=1, keepdims=True)

    tiles = jnp.floor((counts + (ROW_GRAN - 1)) * (1.0 / ROW_GRAN))
    lower = (lax.broadcasted_iota(jnp.int32, (N_EXPERTS, N_EXPERTS), 1)
             < lax.broadcasted_iota(jnp.int32, (N_EXPERTS, N_EXPERTS), 0)).astype(BF16)
    tiles_b = jnp.broadcast_to(tiles, (N_EXPERTS, LANES))
    row_start = ROW_GRAN * jnp.dot(lower, tiles_b.astype(BF16), preferred_element_type=F32)
    meta_ref[0] = row_start.astype(jnp.int32)
    meta_ref[1] = (ROW_GRAN * tiles_b).astype(jnp.int32)

    rs = row_start[:, 0:1]
    for c in range(nc):
        e0, e1, _ = member(c)
        where = rs + rank_scr[c]
        for k, ek in enumerate((e0, e1)):
            pos = jnp.sum(jnp.where(row == ek, where, 0.0), axis=0, keepdims=True).astype(jnp.int32)
            addr = (pos >> GRAN_SHIFT) * TILE_ROWS + (pos & (ROW_GRAN - 1))
            cols = slice(k * MOE_BLOCK + c * tm, k * MOE_BLOCK + (c + 1) * tm)
            addr_ref[:, cols] = addr
            gw_ref[:, cols] = wts_ref[c, k:k + 1, :]


def _plan(ids, wts):
    tm = ROUTER_TM
    nc = MOE_BLOCK // tm
    nb = TOKENS // MOE_BLOCK
    return pl.pallas_call(
        _plan_kernel,
        out_shape=(jax.ShapeDtypeStruct((nb, 1, TOP_K * MOE_BLOCK), jnp.int32),
                   jax.ShapeDtypeStruct((nb, 1, TOP_K * MOE_BLOCK), F32),
                   jax.ShapeDtypeStruct((nb, 2, N_EXPERTS, LANES), jnp.int32)),
        grid=(nb,),
        in_specs=[pl.BlockSpec((nc, SUBLANES, tm), lambda b: (b, 0, 0)),
                  pl.BlockSpec((nc, SUBLANES, tm), lambda b: (b, 0, 0))],
        out_specs=(pl.BlockSpec((None, 1, TOP_K * MOE_BLOCK), lambda b: (b, 0, 0)),
                   pl.BlockSpec((None, 1, TOP_K * MOE_BLOCK), lambda b: (b, 0, 0)),
                   pl.BlockSpec((None, 2, N_EXPERTS, LANES), lambda b: (b, 0, 0, 0))),
        scratch_shapes=[pltpu.VMEM((nc, N_EXPERTS, tm), F32)],
        compiler_params=_params(("arbitrary",)),
        name="moe_plan",
    )(ids, wts)


def _expert_rows(xy_ref, w13_ref, w2_ref, off, n_tiles):
    tile0 = lax.shift_right_logical(off, GRAN_SHIFT)
    bases = [pl.multiple_of((tile0 + ti) * TILE_ROWS, SUBLANES) for ti in range(n_tiles)]
    lo_cols, hi_cols = [], []
    for j in range(PACK_ROWS):
        words = jnp.concatenate(
            [xy_ref[pl.ds(b + j * TILE_STRIDE, ROW_GRAN), :] for b in bases], axis=0)
        lo, hi = _unpack_words(words)
        lo_cols.append(lo.astype(BF16))
        hi_cols.append(hi.astype(BF16))
    xin = jnp.concatenate(lo_cols + hi_cols, axis=1)
    u = jnp.dot(xin, w13_ref[...], preferred_element_type=F32)
    u1 = u[:, :D_EXPERT]
    act = (u1 * jax.nn.sigmoid(u1) * u[:, D_EXPERT:]).astype(BF16)
    y = jnp.dot(act, w2_ref[...], preferred_element_type=F32)
    half = D_MODEL // 2
    for j in range(PACK_ROWS):
        words = _pack_words(y[:, j * LANES:(j + 1) * LANES],
                            y[:, half + j * LANES:half + (j + 1) * LANES])
        for ti, b in enumerate(bases):
            xy_ref[pl.ds(b + j * TILE_STRIDE, ROW_GRAN), :] = words[ti * ROW_GRAN:(ti + 1) * ROW_GRAN]


def _moe_kernel(rs_ref, pad_ref, hfp_ref, addr_ref, gw_ref, w13_ref, w2_ref, x_ref, mod_ref,
                fg_ref, o_ref, xy_scr, out_scr, *, final_norm):
    b = pl.program_id(0)
    s = pl.program_id(1)
    unroll = SUBLANES

    @pl.when((b == 0) & (s == 0))
    def _():
        xy_scr[...] = jnp.zeros(xy_scr.shape, jnp.uint32)

    @pl.when(s == 0)
    def _():
        def group(g, carry):
            for i in range(unroll):
                t = g * unroll + i
                slab = hfp_ref[pl.ds(pl.multiple_of(g * (unroll * PACK_ROWS), unroll * PACK_ROWS)
                                     + i * PACK_ROWS, PACK_ROWS), :]
                for k in range(TOP_K):
                    xy_scr[pl.ds(addr_ref[0, k * MOE_BLOCK + t], PACK_ROWS, stride=TILE_STRIDE), :] = slab
            return carry
        lax.fori_loop(0, MOE_BLOCK // unroll, group, 0)

    def run_expert(slot):
        e = b * N_EXPERTS + s * EXPERTS_PER_STEP + slot
        row0 = rs_ref[e]
        padded = pad_ref[e]

        def super_tile(i, carry):
            off = row0 + i * SUPER_ROWS
            n_tiles = lax.shift_right_logical(
                jnp.minimum(padded - i * SUPER_ROWS, SUPER_ROWS), GRAN_SHIFT)
            for m in range(1, SUPER_ROWS // ROW_GRAN + 1):
                @pl.when(n_tiles == m)
                def _(m=m):
                    _expert_rows(xy_scr, w13_ref.at[slot], w2_ref.at[slot], off, m)
            return carry
        lax.fori_loop(0, lax.shift_right_logical(padded + (SUPER_ROWS - 1), SUPER_SHIFT), super_tile, 0)

    @pl.when(s < N_EXPERT_STEPS)
    def _():
        for slot in range(EXPERTS_PER_STEP):
            run_expert(slot)

    @pl.when(s >= N_EXPERT_STEPS)
    def _():
        t0 = (s - N_EXPERT_STEPS) * EPI_ROWS

        def group(g, carry):
            for i in range(unroll):
                tl = g * unroll + i
                lo_acc = hi_acc = None
                for k in range(TOP_K):
                    pair = k * MOE_BLOCK + t0 + tl
                    words = xy_scr[pl.ds(addr_ref[0, pair], PACK_ROWS, stride=TILE_STRIDE), :]
                    lo, hi = _unpack_words(words)
                    wk = gw_ref[0, pair]
                    lo_acc = wk * lo if lo_acc is None else lo_acc + wk * lo
                    hi_acc = wk * hi if hi_acc is None else hi_acc + wk * hi
                out_scr[pl.ds(tl, PACK_ROWS, stride=OUT_PITCH), :] = lo_acc
                out_scr[pl.ds(PACK_ROWS * OUT_PITCH + tl, PACK_ROWS, stride=OUT_PITCH), :] = hi_acc
            return carry
        lax.fori_loop(0, EPI_ROWS // unroll, group, 0)

        for j in range(D_MODEL // LANES):
            cols = slice(j * LANES, (j + 1) * LANES)
            moe = out_scr[j * OUT_PITCH:j * OUT_PITCH + EPI_ROWS, :]
            o_ref[:, cols] = x_ref[:, cols] + _row(mod_ref, FFN_GATE, cols) * moe
        if final_norm:
            v = o_ref[...]
            o_ref[...] = v * lax.rsqrt(jnp.mean(v * v, axis=-1, keepdims=True) + NORM_EPS) * fg_ref[...]


def _moe_sparse(hfp, addr, gw, row_start, padded, w13, w2, x, mod, final_g, *, layer, final_norm):
    nb = TOKENS // MOE_BLOCK
    steps = N_EXPERT_STEPS + N_EPI

    def epi_block(b, s, *_):
        return (b * N_EPI + jnp.maximum(s - N_EXPERT_STEPS, 0), 0)

    def expert_block(b, s, *_):
        return (layer, jnp.minimum(s, N_EXPERT_STEPS - 1), 0, 0)

    grid_spec = pltpu.PrefetchScalarGridSpec(
        num_scalar_prefetch=2,
        grid=(nb, steps),
        in_specs=[
            pl.BlockSpec((MOE_BLOCK * PACK_ROWS, LANES), lambda b, s, *_: (b, 0)),
            pl.BlockSpec((None, 1, TOP_K * MOE_BLOCK), lambda b, s, *_: (b, 0, 0),
                         memory_space=pltpu.SMEM),
            pl.BlockSpec((None, 1, TOP_K * MOE_BLOCK), lambda b, s, *_: (b, 0, 0),
                         memory_space=pltpu.SMEM),
            pl.BlockSpec((None, EXPERTS_PER_STEP, D_MODEL, 2 * D_EXPERT), expert_block),
            pl.BlockSpec((None, EXPERTS_PER_STEP, D_EXPERT, D_MODEL), expert_block),
            pl.BlockSpec((EPI_ROWS, D_MODEL), epi_block),
            pl.BlockSpec((None, 6, D_MODEL), lambda b, s, *_: (b, 0, 0)),
            pl.BlockSpec((1, D_MODEL), lambda b, s, *_: (0, 0)),
        ],
        out_specs=pl.BlockSpec((EPI_ROWS, D_MODEL), epi_block),
        scratch_shapes=[
            pltpu.VMEM((N_TILES * TILE_ROWS, LANES), jnp.uint32),
            pltpu.VMEM((D_MODEL // LANES * OUT_PITCH, LANES), F32),
        ],
    )
    return pl.pallas_call(
        functools.partial(_moe_kernel, final_norm=final_norm),
        out_shape=jax.ShapeDtypeStruct((TOKENS, D_MODEL), F32),
        grid_spec=grid_spec,
        compiler_params=_params(("arbitrary", "arbitrary")),
        name="moe_experts",
    )(row_start, padded, hfp, addr, gw, w13, w2, x, mod,
      final_g.reshape(1, D_MODEL))


def _lambda_init(layer):
    return 0.8 - 0.6 * math.exp(-0.3 * layer)


def _split_bf16(w):
    hi = w.astype(BF16)
    return hi, (w - hi.astype(F32)).astype(BF16)


def kernel(x, c, norm_mix, norm_ffn, final_norm, ada_w, ada_b, attn_w_qkv, attn_lambda, attn_subln, attn_w_o, rec_w_in, rec_conv_w, rec_conv_b, rec_gate_w, rec_gate_b, rec_a_param, rec_w_out, moe_w_group, moe_b_group, moe_w_expert, moe_b_expert, moe_w13, moe_w2):
    xt = x.reshape(TOKENS, D_MODEL)
    mod_all = _ada_mod(c, ada_w, ada_b).reshape(DEPTH, BATCH, 6, D_MODEL)

    def lane_pad(n, *lead):
        return jnp.zeros(lead + (n,), F32)

    gap = EXPERT_ROW0 - N_GROUPS
    tail = ROUTER_LANES - EXPERT_ROW0 - N_EXPERTS
    w_router = jnp.concatenate([moe_w_group, lane_pad(gap, DEPTH, D_MODEL), moe_w_expert,
                                lane_pad(tail, DEPTH, D_MODEL)], axis=-1)
    b_router = jnp.concatenate([moe_b_group, lane_pad(gap, DEPTH), moe_b_expert,
                                lane_pad(tail, DEPTH)], axis=-1)
    wr_hi, wr_lo = _split_bf16(w_router)

    w_qkv = attn_w_qkv.astype(BF16)
    w_o = attn_w_o.astype(BF16)
    w_in = rec_w_in.astype(BF16)
    w_gate = rec_gate_w.astype(BF16)
    w_out = rec_w_out.astype(BF16)
    w13 = moe_w13.astype(BF16)
    w2 = moe_w2.astype(BF16)

    for layer in range(DEPTH):
        mod = mod_all[layer]
        j = layer // N_MIXERS
        route = (norm_ffn[layer], wr_hi[layer], wr_lo[layer], b_router[layer].reshape(1, ROUTER_LANES))
        if layer % N_MIXERS == 0:
            qkv = _norm_proj(xt, norm_mix[layer], mod, w_qkv[j], out_scales=(Q_PRESCALE, 1.0, 1.0))
            o = _diff_attention(qkv, attn_lambda[j], attn_subln[j], _lambda_init(layer))
            xt, hfp, ids, wts = _proj_residual(o, w_o[j], xt, mod, *route)
        else:
            xt, hfp, ids, wts = _recurrent_layer(
                xt, norm_mix[layer], mod, w_in[j], rec_conv_w[j], rec_conv_b[j], w_gate[j],
                rec_gate_b[j], rec_a_param[j], w_out[j], *route)
        addr, gw, meta = _plan(ids, wts)
        xt = _moe_sparse(hfp, addr, gw, meta[:, 0, :, 0].reshape(-1), meta[:, 1, :, 0].reshape(-1),
                         w13, w2, xt, mod, final_norm,
                         layer=layer, final_norm=(layer == DEPTH - 1))

    return xt.reshape(BATCH, SEQ, D_MODEL)
```
